```python
import jax, jax.numpy as jnp
from jax import lax
import numpy as np

D_MODEL = 2048
BATCH = 4
SEQ = 2048
DEPTH = 1
DEC_BATCH = 8
DEC_SEQ = 16
PAST_LEN = 1024

CHUNK = 64
HEAD_DIM = 128
A_HEADS = 8
A_LEFT_CHUNKS = 8
A_WINDOW = A_LEFT_CHUNKS * CHUNK
REL_CLIP = 128
B_HEADS = 8
FOX_QBLOCK = 128
A_WIDTH = A_HEADS * HEAD_DIM
B_WIDTH = B_HEADS * HEAD_DIM
PEER_HEADS = 8
PEER_NKEYS = 128
PEER_EXPERTS = PEER_NKEYS * PEER_NKEYS
PEER_TOPK = 16
PEER_QDIM = 256
PEER_HALF = PEER_QDIM // 2
PEER_TOK_BLOCK = 64
FORGET_BIAS_INIT = 3.0
EPS = 1e-6
IN_SPLITS = (A_WIDTH, A_WIDTH, A_WIDTH, B_WIDTH, B_WIDTH, B_WIDTH, B_HEADS, D_MODEL, D_MODEL)
IN_WIDTH = sum(IN_SPLITS)
ATT_SCALE = HEAD_DIM ** -0.5

kernel_name = "hybrid_stream_chunkband_fox_peer_step"


def rmsnorm(x, g):
    xf = x.astype(jnp.float32)
    y = xf * lax.rsqrt(jnp.mean(xf * xf, axis=-1, keepdims=True) + EPS)
    return (y * g.astype(jnp.float32)).astype(x.dtype)


def in_proj(x, ln1, w_in, b_f, qn_a, kn_a, qn_b, kn_b):
    bsz, t = x.shape[0], x.shape[1]
    p = rmsnorm(x, ln1) @ w_in
    qa, ka, va, qb, kb, vb, f, ga, gb = jnp.split(p, np.cumsum(IN_SPLITS)[:-1].tolist(), axis=-1)
    heads = lambda z, h: z.reshape(bsz, t, h, HEAD_DIM)
    qa = rmsnorm(heads(qa, A_HEADS), qn_a)
    ka = rmsnorm(heads(ka, A_HEADS), kn_a)
    qb = rmsnorm(heads(qb, B_HEADS), qn_b)
    kb = rmsnorm(heads(kb, B_HEADS), kn_b)
    lf = jax.nn.log_sigmoid(f.astype(jnp.float32) + b_f.astype(jnp.float32))
    return (qa, ka, heads(va, A_HEADS), qb, kb, heads(vb, B_HEADS), lf,
            jax.nn.sigmoid(ga), jax.nn.sigmoid(gb))


def band_attention(q, k, v, qpos, kpos, kvalid, rel_table):
    s = jnp.einsum("bnqhd,bnkhd->bhnqk", q, k).astype(jnp.float32) * ATT_SCALE
    rel = jnp.clip(qpos[:, :, None] - kpos[:, None, :], -REL_CLIP, REL_CLIP) + REL_CLIP
    s = s + rel_table.astype(jnp.float32)[:, rel][None]
    s = jnp.where(kvalid[None, None, :, None, :], s, -jnp.inf)
    p = jax.nn.softmax(s, axis=-1).astype(v.dtype)
    return jnp.einsum("bhnqk,bnkhd->bnqhd", p, v)


def chunk_band_prompt(q, k, v, rel_table):
    bsz, s, h, d = q.shape
    nc = s // CHUNK
    kw = A_WINDOW + CHUNK
    idx = jnp.arange(nc)[:, None] * CHUNK + jnp.arange(kw)[None, :]
    kpos = idx - A_WINDOW
    pad = ((0, 0), (A_WINDOW, 0), (0, 0), (0, 0))
    kband = jnp.pad(k, pad)[:, idx]
    vband = jnp.pad(v, pad)[:, idx]
    qc = q.reshape(bsz, nc, CHUNK, h, d)
    qpos = jnp.arange(s).reshape(nc, CHUNK)
    o = band_attention(qc, kband, vband, qpos, kpos, kpos >= 0, rel_table)
    return o.reshape(bsz, s, h, d)


def chunk_band_sample(q, k, v, cache_k, cache_v, rel_table, past_len):
    t = q.shape[1]
    lc = cache_k.shape[1]
    kk = jnp.concatenate([cache_k.astype(k.dtype), k], axis=1)
    vv = jnp.concatenate([cache_v.astype(v.dtype), v], axis=1)
    qpos = (past_len + jnp.arange(t))[None]
    kpos = jnp.concatenate([past_len - lc + jnp.arange(lc), past_len + jnp.arange(t)])[None]
    valid = jnp.ones(kpos.shape, dtype=bool)
    o = band_attention(q[:, None], kk[:, None], vv[:, None], qpos, kpos, valid, rel_table)
    return o[:, 0]


def fox_core(q, cq, qpos, k, v, ck, kpos):
    s = jnp.einsum("bqhd,bkhd->bhqk", q, k).astype(jnp.float32) * ATT_SCALE
    s = s + jnp.transpose(cq, (0, 2, 1))[..., :, None] - jnp.transpose(ck, (0, 2, 1))[..., None, :]
    s = jnp.where((kpos[None, :] <= qpos[:, None])[None, None], s, -jnp.inf)
    p = jax.nn.softmax(s, axis=-1).astype(v.dtype)
    return jnp.einsum("bhqk,bkhd->bqhd", p, v)


def fox_prompt(q, k, v, lf):
    bsz, s, h, d = q.shape
    c = jnp.cumsum(lf, axis=1)
    nb = s // FOX_QBLOCK
    qb = q.reshape(bsz, nb, FOX_QBLOCK, h, d).swapaxes(0, 1)
    cqb = c.reshape(bsz, nb, FOX_QBLOCK, h).swapaxes(0, 1)
    posb = jnp.arange(s).reshape(nb, FOX_QBLOCK)
    kpos = jnp.arange(s)
    out = lax.map(lambda a: fox_core(a[0], a[1], a[2], k, v, c, kpos), (qb, cqb, posb))
    return out.swapaxes(0, 1).reshape(bsz, s, h, d)


def fox_sample(q, k, v, lf, cache_k, cache_v, cache_lf, past_len):
    t = q.shape[1]
    kk = jnp.concatenate([cache_k.astype(k.dtype), k], axis=1)
    vv = jnp.concatenate([cache_v.astype(v.dtype), v], axis=1)
    c = jnp.cumsum(jnp.concatenate([cache_lf.astype(jnp.float32), lf], axis=1), axis=1)
    qpos = past_len + jnp.arange(t)
    kpos = jnp.arange(past_len + t)
    return fox_core(q, c[:, past_len:], qpos, kk, vv, c, kpos)


def peer(x2d, w_pq, sub_keys, peer_u, peer_v):
    n = x2d.shape[0]
    npad = (-n) % PEER_TOK_BLOCK
    xp = jnp.pad(x2d, ((0, npad), (0, 0)))
    m = xp.shape[0]
    q = (xp @ w_pq).reshape(m, PEER_HEADS, 2, PEER_HALF)
    s = jnp.einsum("thpc,pnc->thpn", q, sub_keys).astype(jnp.float32)
    hs, hi = lax.top_k(s, PEER_TOPK)
    cand_s = (hs[:, :, 0, :, None] + hs[:, :, 1, None, :]).reshape(m, PEER_HEADS, PEER_TOPK * PEER_TOPK)
    cand_i = (hi[:, :, 0, :, None] * PEER_NKEYS + hi[:, :, 1, None, :]).reshape(m, PEER_HEADS, PEER_TOPK * PEER_TOPK)
    best_s, pos = lax.top_k(cand_s, PEER_TOPK)
    idx = jnp.take_along_axis(cand_i, pos, axis=-1)
    gate = jax.nn.softmax(best_s, axis=-1).astype(x2d.dtype)
    nb = m // PEER_TOK_BLOCK

    def experts(args):
        xb, ib, gb = args
        a = jnp.einsum("thkd,td->thk", peer_u[ib], xb)
        act = jax.nn.gelu(a, approximate=False) * gb
        return jnp.einsum("thk,thkd->td", act, peer_v[ib])

    out = lax.map(experts, (xp.reshape(nb, PEER_TOK_BLOCK, D_MODEL),
                            idx.reshape(nb, PEER_TOK_BLOCK, PEER_HEADS, PEER_TOPK),
                            gate.reshape(nb, PEER_TOK_BLOCK, PEER_HEADS, PEER_TOPK)))
    return out.reshape(m, D_MODEL)[:n]


def merge_ffn(x, ya, yb, ga, gb, w_br_a, w_br_b, w_out, ln2, w_pq, sub_keys, peer_u, peer_v):
    bsz, t = x.shape[0], x.shape[1]
    m = ga * (ya.reshape(bsz, t, A_WIDTH) @ w_br_a) + gb * (yb.reshape(bsz, t, B_WIDTH) @ w_br_b)
    h = x + m @ w_out
    f = peer(rmsnorm(h, ln2).reshape(bsz * t, D_MODEL), w_pq, sub_keys, peer_u, peer_v)
    return h + f.reshape(bsz, t, D_MODEL)


def setup_inputs(seed: int = 0) -> dict:
    key = jax.random.key(seed)
    ks = jax.random.split(key, 24)
    nrm = lambda k, shape, scale: jax.random.normal(k, shape, jnp.float32) * scale
    la = min(A_WINDOW, PAST_LEN)
    return {
        "x_prompt": nrm(ks[0], (BATCH, SEQ, D_MODEL), 1.0),
        "x_sample": nrm(ks[1], (DEC_BATCH, DEC_SEQ, D_MODEL), 1.0),
        "cache_a_k": nrm(ks[2], (DEPTH, DEC_BATCH, la, A_HEADS, HEAD_DIM), 1.0),
        "cache_a_v": nrm(ks[3], (DEPTH, DEC_BATCH, la, A_HEADS, HEAD_DIM), 1.0),
        "cache_b_k": nrm(ks[4], (DEPTH, DEC_BATCH, PAST_LEN, B_HEADS, HEAD_DIM), 1.0),
        "cache_b_v": nrm(ks[5], (DEPTH, DEC_BATCH, PAST_LEN, B_HEADS, HEAD_DIM), 1.0),
        "cache_b_lf": jax.nn.log_sigmoid(FORGET_BIAS_INIT + nrm(ks[6], (DEPTH, DEC_BATCH, PAST_LEN, B_HEADS), 1.0)),
        "ln1": 1.0 + nrm(ks[7], (DEPTH, D_MODEL), 0.02),
        "w_in": nrm(ks[8], (DEPTH, D_MODEL, IN_WIDTH), D_MODEL ** -0.5),
        "b_f": FORGET_BIAS_INIT + nrm(ks[9], (DEPTH, B_HEADS), 0.5),
        "qn_a": 1.0 + nrm(ks[10], (DEPTH, HEAD_DIM), 0.02),
        "kn_a": 1.0 + nrm(ks[11], (DEPTH, HEAD_DIM), 0.02),
        "qn_b": 1.0 + nrm(ks[12], (DEPTH, HEAD_DIM), 0.02),
        "kn_b": 1.0 + nrm(ks[13], (DEPTH, HEAD_DIM), 0.02),
        "rel_bias": nrm(ks[14], (DEPTH, A_HEADS, 2 * REL_CLIP + 1), 0.1),
        "w_br_a": nrm(ks[15], (DEPTH, A_WIDTH, D_MODEL), A_WIDTH ** -0.5),
        "w_br_b": nrm(ks[16], (DEPTH, B_WIDTH, D_MODEL), B_WIDTH ** -0.5),
        "w_out": nrm(ks[17], (DEPTH, D_MODEL, D_MODEL), D_MODEL ** -0.5),
        "ln2": 1.0 + nrm(ks[18], (DEPTH, D_MODEL), 0.02),
        "w_pq": nrm(ks[19], (DEPTH, D_MODEL, PEER_HEADS * PEER_QDIM), D_MODEL ** -0.5),
        "sub_keys": nrm(ks[20], (DEPTH, 2, PEER_NKEYS, PEER_HALF), PEER_HALF ** -0.5),
        "peer_u": nrm(ks[21], (DEPTH, PEER_EXPERTS, D_MODEL), D_MODEL ** -0.5),
        "peer_v": nrm(ks[22], (DEPTH, PEER_EXPERTS, D_MODEL), PEER_HEADS ** -0.5),
    }


def reference(x_prompt, x_sample, cache_a_k, cache_a_v, cache_b_k, cache_b_v, cache_b_lf,
              ln1, w_in, b_f, qn_a, kn_a, qn_b, kn_b, rel_bias, w_br_a, w_br_b, w_out,
              ln2, w_pq, sub_keys, peer_u, peer_v):
    past_len = cache_b_k.shape[2]
    keep = min(A_WINDOW, x_prompt.shape[1])
    hp, hs = x_prompt, x_sample
    akp, avp, aks, avs, bkp, bvp, blp, bks, bvs, bls = ([] for _ in range(10))
    for l in range(DEPTH):
        qa, ka, va, qb, kb, vb, lf, ga, gb = in_proj(hp, ln1[l], w_in[l], b_f[l], qn_a[l], kn_a[l], qn_b[l], kn_b[l])
        ya = chunk_band_prompt(qa, ka, va, rel_bias[l])
        yb = fox_prompt(qb, kb, vb, lf)
        akp.append(ka[:, -keep:]); avp.append(va[:, -keep:])
        bkp.append(kb); bvp.append(vb); blp.append(lf)
        hp = merge_ffn(hp, ya, yb, ga, gb, w_br_a[l], w_br_b[l], w_out[l], ln2[l], w_pq[l], sub_keys[l], peer_u[l], peer_v[l])
        qa, ka, va, qb, kb, vb, lf, ga, gb = in_proj(hs, ln1[l], w_in[l], b_f[l], qn_a[l], kn_a[l], qn_b[l], kn_b[l])
        ya = chunk_band_sample(qa, ka, va, cache_a_k[l], cache_a_v[l], rel_bias[l], past_len)
        yb = fox_sample(qb, kb, vb, lf, cache_b_k[l], cache_b_v[l], cache_b_lf[l], past_len)
        aks.append(ka); avs.append(va)
        bks.append(kb); bvs.append(vb); bls.append(lf)
        hs = merge_ffn(hs, ya, yb, ga, gb, w_br_a[l], w_br_b[l], w_out[l], ln2[l], w_pq[l], sub_keys[l], peer_u[l], peer_v[l])
    return (hp, hs,
            jnp.stack(akp), jnp.stack(avp), jnp.stack(aks), jnp.stack(avs),
            jnp.stack(bkp), jnp.stack(bvp), jnp.stack(blp),
            jnp.stack(bks), jnp.stack(bvs), jnp.stack(bls))
```

```python
import functools

import numpy as np
import jax
import jax.numpy as jnp
from jax import lax
from jax.experimental import pallas as pl
from jax.experimental.pallas import tpu as pltpu

EPS = 1e-6
CHUNK = 64
HEAD_DIM = 128
LEFT_CHUNKS = 8
WINDOW = LEFT_CHUNKS * CHUNK
REL_CLIP = 128
PEER_HEADS = 8
PEER_NKEYS = 128
PEER_TOPK = 16
PEER_HALF = 128
ATT_SCALE = HEAD_DIM ** -0.5
NEG = -1e30
INV_SQRT2 = 0.7071067811865476
LANES = 128
SUBLANES = 8
VMEM_LIMIT = 48 * 1024 * 1024

F32 = jnp.float32
BF16 = jnp.bfloat16
NT_DIMS = (((1,), (1,)), ((), ()))


def _params(*sem):
    return pltpu.CompilerParams(dimension_semantics=sem, vmem_limit_bytes=VMEM_LIMIT)


def _row_tile(m, pref=512):
    return pref if m % pref == 0 else m


def _rmsnorm_kernel(x_ref, g_ref, o_ref):
    x = x_ref[...]
    ms = jnp.mean(x * x, axis=-1, keepdims=True)
    o_ref[...] = (x * lax.rsqrt(ms + EPS) * g_ref[...]).astype(o_ref.dtype)


def _rmsnorm_bf16(x2d, g):
    m, d = x2d.shape
    tm = _row_tile(m)
    return pl.pallas_call(
        _rmsnorm_kernel,
        grid=(m // tm,),
        in_specs=[pl.BlockSpec((tm, d), lambda i: (i, 0)), pl.BlockSpec((1, d), lambda i: (0, 0))],
        out_specs=pl.BlockSpec((tm, d), lambda i: (i, 0)),
        out_shape=jax.ShapeDtypeStruct((m, d), BF16),
        compiler_params=_params("parallel"),
    )(x2d, g.reshape(1, d))


def _proj_kernel(x_ref, w_ref, aux_ref, *out_refs, epilogue):
    acc = jnp.dot(x_ref[...], w_ref[...], preferred_element_type=F32)
    tn = acc.shape[1]
    if epilogue == "headnorm":
        for c in range(tn // HEAD_DIM):
            sl = slice(c * HEAD_DIM, (c + 1) * HEAD_DIM)
            y = acc[:, sl]
            ms = jnp.mean(y * y, axis=-1, keepdims=True)
            r = y * lax.rsqrt(ms + EPS) * aux_ref[:, sl]
            for o in out_refs:
                o[:, sl] = r.astype(o.dtype)
        return
    if epilogue == "sigmoid":
        res = jax.nn.sigmoid(acc)
    elif epilogue == "logsigmoid":
        z = acc + aux_ref[...]
        res = jnp.minimum(z, 0.0) - jnp.log1p(jnp.exp(-jnp.abs(z)))
    else:
        res = acc
    for o in out_refs:
        o[...] = res.astype(o.dtype)


def _proj(xn, w, aux, epilogue, out_dtypes):
    m, k = xn.shape
    n = w.shape[1]
    tm = _row_tile(m)
    tn = min(512, n)
    outs = pl.pallas_call(
        functools.partial(_proj_kernel, epilogue=epilogue),
        grid=(m // tm, n // tn),
        in_specs=[
            pl.BlockSpec((tm, k), lambda i, j: (i, 0)),
            pl.BlockSpec((k, tn), lambda i, j: (0, j)),
            pl.BlockSpec((1, tn), lambda i, j: (0, j)),
        ],
        out_specs=[pl.BlockSpec((tm, tn), lambda i, j: (i, j)) for _ in out_dtypes],
        out_shape=[jax.ShapeDtypeStruct((m, n), dt) for dt in out_dtypes],
        compiler_params=_params("parallel", "arbitrary"),
    )(xn, w, aux)
    return outs


def _cumsum_kernel(x_ref, o_ref):
    x = x_ref[0]
    n = x.shape[1]
    lane = lax.broadcasted_iota(jnp.int32, x.shape, 1)
    s = 1
    while s < n:
        x = x + jnp.where(lane >= s, pltpu.roll(x, s, 1), 0.0)
        s *= 2
    o_ref[0] = x


def _cumsum_lanes(x):
    b, h, n = x.shape
    return pl.pallas_call(
        _cumsum_kernel,
        grid=(b,),
        in_specs=[pl.BlockSpec((1, h, n), lambda i: (i, 0, 0))],
        out_specs=pl.BlockSpec((1, h, n), lambda i: (i, 0, 0)),
        out_shape=jax.ShapeDtypeStruct((b, h, n), F32),
        compiler_params=_params("parallel"),
    )(x)


def _relbias_kernel(tab_ref, rel_ref, o_ref):
    rel = rel_ref[...]
    nh, nr = tab_ref.shape
    for h in range(nh):
        def body(r, acc, h=h):
            return jnp.where(rel == r, tab_ref[h, r], acc)
        o_ref[h] = lax.fori_loop(0, nr, body, jnp.zeros(rel.shape, F32))


def _relbias(table, rel):
    nh = table.shape[0]
    q, k = rel.shape
    return pl.pallas_call(
        _relbias_kernel,
        in_specs=[pl.BlockSpec(memory_space=pltpu.SMEM), pl.BlockSpec((q, k), lambda: (0, 0))],
        out_specs=pl.BlockSpec((nh, q, k), lambda: (0, 0, 0)),
        out_shape=jax.ShapeDtypeStruct((nh, q, k), F32),
    )(table, jnp.asarray(rel, jnp.int32))


def _band_prompt_kernel(q_ref, k_ref, v_ref, bias_ref, o_ref, kpad, vpad):
    s_len = q_ref.shape[1]
    kw_len = WINDOW + CHUNK
    kpad[0:WINDOW, :] = jnp.zeros((WINDOW, HEAD_DIM), kpad.dtype)
    vpad[0:WINDOW, :] = jnp.zeros((WINDOW, HEAD_DIM), vpad.dtype)
    kpad[WINDOW:, :] = k_ref[0]
    vpad[WINDOW:, :] = v_ref[0]
    bias = bias_ref[0]
    kk = lax.broadcasted_iota(jnp.int32, (CHUNK, kw_len), 1)

    def body(n, carry):
        start = pl.multiple_of(n * CHUNK, CHUNK)
        q = q_ref[0, pl.ds(start, CHUNK), :]
        kw = kpad[pl.ds(start, kw_len), :]
        vw = vpad[pl.ds(start, kw_len), :]
        s = lax.dot_general(q, kw, NT_DIMS, preferred_element_type=F32) * ATT_SCALE + bias
        s = jnp.where(kk >= WINDOW - start, s, NEG)
        m = jnp.max(s, axis=-1, keepdims=True)
        p = jnp.exp(s - m)
        l = jnp.sum(p, axis=-1, keepdims=True)
        o = jnp.dot(p.astype(BF16), vw, preferred_element_type=F32) / l
        o_ref[0, pl.ds(start, CHUNK), :] = o.astype(o_ref.dtype)
        return carry

    lax.fori_loop(0, s_len // CHUNK, body, 0)


def _band_prompt(q, k, v, bias, nheads):
    b, s, _ = q.shape
    blk = pl.BlockSpec((1, s, HEAD_DIM), lambda i, h: (i, 0, h))
    return pl.pallas_call(
        _band_prompt_kernel,
        grid=(b, nheads),
        in_specs=[blk, blk, blk,
                  pl.BlockSpec((1, CHUNK, WINDOW + CHUNK), lambda i, h: (h, 0, 0))],
        out_specs=blk,
        out_shape=jax.ShapeDtypeStruct((b, s, nheads * HEAD_DIM), BF16),
        scratch_shapes=[pltpu.VMEM((WINDOW + s, HEAD_DIM), BF16),
                        pltpu.VMEM((WINDOW + s, HEAD_DIM), BF16)],
        compiler_params=_params("parallel", "parallel"),
    )(q, k, v, bias)


def _fox_prompt_kernel(q_ref, k_ref, v_ref, cq_ref, ck_ref, o_ref, m_sc, l_sc, acc_sc):
    qi = pl.program_id(2)
    ki = pl.program_id(3)
    tq = q_ref.shape[1]
    tk = k_ref.shape[1]

    @pl.when(ki == 0)
    def _init():
        m_sc[...] = jnp.full(m_sc.shape, NEG, F32)
        l_sc[...] = jnp.zeros(l_sc.shape, F32)
        acc_sc[...] = jnp.zeros(acc_sc.shape, F32)

    def step(diagonal):
        s = lax.dot_general(q_ref[0], k_ref[0], NT_DIMS, preferred_element_type=F32) * ATT_SCALE
        s = s + cq_ref[0] - ck_ref[0]
        if diagonal:
            row = lax.broadcasted_iota(jnp.int32, (tq, tk), 0)
            col = lax.broadcasted_iota(jnp.int32, (tq, tk), 1)
            s = jnp.where(col <= row, s, NEG)
        m_prev = m_sc[...]
        m_new = jnp.maximum(m_prev, jnp.max(s, axis=-1, keepdims=True))
        alpha = jnp.exp(m_prev - m_new)
        p = jnp.exp(s - m_new)
        l_sc[...] = alpha * l_sc[...] + jnp.sum(p, axis=-1, keepdims=True)
        acc_sc[...] = alpha * acc_sc[...] + jnp.dot(p.astype(BF16), v_ref[0], preferred_element_type=F32)
        m_sc[...] = m_new

    @pl.when(ki < qi)
    def _below():
        step(False)

    @pl.when(ki == qi)
    def _diag():
        step(True)
        o_ref[0] = (acc_sc[...] / l_sc[...]).astype(o_ref.dtype)


def _fox_prompt(q, k, v, c_col, c_row, nheads, head0):
    b, s, _ = q.shape
    t = min(512, s)
    nq = s // t
    return pl.pallas_call(
        _fox_prompt_kernel,
        grid=(b, nheads, nq, nq),
        in_specs=[
            pl.BlockSpec((1, t, HEAD_DIM), lambda i, h, qi, ki: (i, qi, head0 + h)),
            pl.BlockSpec((1, t, HEAD_DIM), lambda i, h, qi, ki: (i, jnp.minimum(ki, qi), head0 + h)),
            pl.BlockSpec((1, t, HEAD_DIM), lambda i, h, qi, ki: (i, jnp.minimum(ki, qi), head0 + h)),
            pl.BlockSpec((1, t, 1), lambda i, h, qi, ki: (i * nheads + h, qi, 0)),
            pl.BlockSpec((1, 1, t), lambda i, h, qi, ki: (i * nheads + h, 0, jnp.minimum(ki, qi))),
        ],
        out_specs=pl.BlockSpec((1, t, HEAD_DIM), lambda i, h, qi, ki: (i, qi, h)),
        out_shape=jax.ShapeDtypeStruct((b, s, nheads * HEAD_DIM), BF16),
        scratch_shapes=[pltpu.VMEM((t, 1), F32), pltpu.VMEM((t, 1), F32), pltpu.VMEM((t, HEAD_DIM), F32)],
        compiler_params=_params("parallel", "parallel", "parallel", "arbitrary"),
    )(q, k, v, c_col, c_row)


def _two_part_attention(qh, kc, vc, kn, vn, add_c, add_n, causal_new):
    sc = lax.dot_general(qh, kc, NT_DIMS, preferred_element_type=F32) * ATT_SCALE + add_c
    sn = lax.dot_general(qh, kn, NT_DIMS, preferred_element_type=F32) * ATT_SCALE + add_n
    if causal_new:
        row = lax.broadcasted_iota(jnp.int32, sn.shape, 0)
        col = lax.broadcasted_iota(jnp.int32, sn.shape, 1)
        sn = jnp.where(col <= row, sn, NEG)
    m = jnp.maximum(jnp.max(sc, axis=-1, keepdims=True), jnp.max(sn, axis=-1, keepdims=True))
    pc = jnp.exp(sc - m)
    pn = jnp.exp(sn - m)
    l = jnp.sum(pc, axis=-1, keepdims=True) + jnp.sum(pn, axis=-1, keepdims=True)
    o = (jnp.dot(pc.astype(BF16), vc, preferred_element_type=F32)
         + jnp.dot(pn.astype(BF16), vn, preferred_element_type=F32))
    return o / l


def _band_sample_kernel(q_ref, kn_ref, vn_ref, kc_ref, vc_ref, bias_ref, o_ref, *, nheads, head0):
    lc = kc_ref.shape[1]
    for h in range(nheads):
        sl = slice((head0 + h) * HEAD_DIM, (head0 + h + 1) * HEAD_DIM)
        cl = slice(h * HEAD_DIM, (h + 1) * HEAD_DIM)
        bias = bias_ref[h]
        o = _two_part_attention(
            q_ref[0, :, sl], kc_ref[0, :, cl].astype(BF16), vc_ref[0, :, cl].astype(BF16),
            kn_ref[0, :, sl], vn_ref[0, :, sl], bias[:, :lc], bias[:, lc:], causal_new=False)
        o_ref[0, :, cl] = o.astype(o_ref.dtype)


def _fox_sample_kernel(q_ref, kn_ref, vn_ref, kc_ref, vc_ref, ccol_ref, crow_ref, o_ref, *, nheads, head0):
    lc = kc_ref.shape[1]
    t = q_ref.shape[1]
    for h in range(nheads):
        sl = slice((head0 + h) * HEAD_DIM, (head0 + h + 1) * HEAD_DIM)
        cl = slice(h * HEAD_DIM, (h + 1) * HEAD_DIM)
        cq = ccol_ref[0, :, h:h + 1]
        o = _two_part_attention(
            q_ref[0, :, sl], kc_ref[0, :, cl].astype(BF16), vc_ref[0, :, cl].astype(BF16),
            kn_ref[0, :, sl], vn_ref[0, :, sl],
            cq - crow_ref[0, h:h + 1, 0:lc], cq - crow_ref[0, h:h + 1, lc:lc + t], causal_new=True)
        o_ref[0, :, cl] = o.astype(o_ref.dtype)


def _sample_attention(kernel_fn, q, kn, vn, kc, vc, extras, extra_specs, nheads):
    b, t, w = q.shape
    lc = kc.shape[1]
    new_blk = pl.BlockSpec((1, t, w), lambda i: (i, 0, 0))
    cache_blk = pl.BlockSpec((1, lc, nheads * HEAD_DIM), lambda i: (i, 0, 0))
    return pl.pallas_call(
        kernel_fn,
        grid=(b,),
        in_specs=[new_blk, new_blk, new_blk, cache_blk, cache_blk] + extra_specs,
        out_specs=pl.BlockSpec((1, t, nheads * HEAD_DIM), lambda i: (i, 0, 0)),
        out_shape=jax.ShapeDtypeStruct((b, t, nheads * HEAD_DIM), BF16),
        compiler_params=_params("parallel"),
    )(q, kn, vn, kc, vc, *extras)


def _merge_kernel(ya_ref, yb_ref, wa_ref, wb_ref, ga_ref, gb_ref, o_ref):
    a = jnp.dot(ya_ref[...], wa_ref[...], preferred_element_type=F32)
    b = jnp.dot(yb_ref[...], wb_ref[...], preferred_element_type=F32)
    o_ref[...] = (ga_ref[...] * a + gb_ref[...] * b).astype(o_ref.dtype)


def _merge(ya, yb, wa, wb, gates):
    m, ka = ya.shape
    n = wa.shape[1]
    tm = _row_tile(m)
    tn = min(512, n)
    nj = n // tn
    return pl.pallas_call(
        _merge_kernel,
        grid=(m // tm, nj),
        in_specs=[
            pl.BlockSpec((tm, ka), lambda i, j: (i, 0)),
            pl.BlockSpec((tm, yb.shape[1]), lambda i, j: (i, 0)),
            pl.BlockSpec((ka, tn), lambda i, j: (0, j)),
            pl.BlockSpec((wb.shape[0], tn), lambda i, j: (0, j)),
            pl.BlockSpec((tm, tn), lambda i, j: (i, j)),
            pl.BlockSpec((tm, tn), lambda i, j: (i, j + nj)),
        ],
        out_specs=pl.BlockSpec((tm, tn), lambda i, j: (i, j)),
        out_shape=jax.ShapeDtypeStruct((m, n), BF16),
        compiler_params=_params("parallel", "arbitrary"),
    )(ya, yb, wa, wb, gates, gates)


def _outproj_kernel(m_ref, w_ref, x_ref, g_ref, h_ref, hn_ref):
    h = x_ref[...] + jnp.dot(m_ref[...], w_ref[...], preferred_element_type=F32)
    h_ref[...] = h
    ms = jnp.mean(h * h, axis=-1, keepdims=True)
    hn_ref[...] = (h * lax.rsqrt(ms + EPS) * g_ref[...]).astype(hn_ref.dtype)


def _outproj(mm, w_out, x2d, ln2):
    m, k = mm.shape
    d = w_out.shape[1]
    tm = _row_tile(m, 256)
    return pl.pallas_call(
        _outproj_kernel,
        grid=(m // tm,),
        in_specs=[
            pl.BlockSpec((tm, k), lambda i: (i, 0)),
            pl.BlockSpec((k, d), lambda i: (0, 0)),
            pl.BlockSpec((tm, d), lambda i: (i, 0)),
            pl.BlockSpec((1, d), lambda i: (0, 0)),
        ],
        out_specs=[pl.BlockSpec((tm, d), lambda i: (i, 0)), pl.BlockSpec((tm, d), lambda i: (i, 0))],
        out_shape=[jax.ShapeDtypeStruct((m, d), F32), jax.ShapeDtypeStruct((m, d), BF16)],
        compiler_params=_params("parallel"),
    )(mm, w_out, x2d, ln2.reshape(1, d))


def _extract_topk(x):
    n = x.shape[0]
    iota = lax.broadcasted_iota(jnp.int32, x.shape, 0).astype(F32)
    rank = jnp.full(x.shape, float(PEER_TOPK), F32)
    vals = []
    for r in range(PEER_TOPK):
        m = jnp.max(x, axis=0, keepdims=True)
        first = jnp.min(jnp.where(x == m, iota, float(n)), axis=0, keepdims=True)
        hit = iota == first
        rank = jnp.where(hit, float(r), rank)
        x = jnp.where(hit, -jnp.inf, x)
        vals.append(m)
    return rank, jnp.concatenate(vals, axis=0)


def _merge_counts(hs0, hs1):
    k = hs0.shape[0]
    a_iota = lax.broadcasted_iota(jnp.int32, hs0.shape, 0).astype(F32)
    count = jnp.zeros(hs0.shape, F32)
    front = hs0 + hs1[0:1]
    best = None
    denom = None
    for r in range(k):
        m = jnp.max(front, axis=0, keepdims=True)
        if r == 0:
            best = m
            denom = jnp.ones_like(m)
        else:
            denom = denom + jnp.exp(m - best)
        first = jnp.min(jnp.where(front == m, a_iota, float(k)), axis=0, keepdims=True)
        hit = a_iota == first
        count = jnp.where(hit, count + 1.0, count)
        nxt = jnp.full(hs0.shape, -jnp.inf, F32)
        for b in range(1, k):
            nxt = jnp.where(count == float(b), hs1[b:b + 1], nxt)
        front = jnp.where(hit, hs0 + nxt, front)
    return count, best, denom


def _router_kernel(hn_ref, wq_ref, sk_ref, n0_ref, r1_ref, w0_ref, w1_ref, q_sc):
    tr = hn_ref.shape[0]
    q_sc[...] = lax.dot_general(wq_ref[...], hn_ref[...], NT_DIMS, preferred_element_type=F32)

    def head(h, carry):
        base = pl.multiple_of(h * (2 * PEER_HALF), 2 * PEER_HALF)
        row0 = pl.multiple_of(h * PEER_NKEYS, PEER_NKEYS)
        s0 = jnp.dot(sk_ref[0], q_sc[pl.ds(base, PEER_HALF), :].astype(BF16), preferred_element_type=F32)
        s1 = jnp.dot(sk_ref[1], q_sc[pl.ds(base + PEER_HALF, PEER_HALF), :].astype(BF16),
                     preferred_element_type=F32)
        for g in range(tr // LANES):
            ls = slice(g * LANES, (g + 1) * LANES)
            x0 = s0[:, ls]
            x1 = s1[:, ls]
            rank0, hs0 = _extract_topk(x0)
            rank1, hs1 = _extract_topk(x1)
            count, _, denom = _merge_counts(hs0, hs1)
            n0 = jnp.zeros(x0.shape, F32)
            for a in range(PEER_TOPK):
                n0 = jnp.where(rank0 == float(a), count[a:a + 1], n0)
            n0_ref[pl.ds(row0, PEER_NKEYS), ls] = n0
            r1_ref[pl.ds(row0, PEER_NKEYS), ls] = rank1
            w0_ref[pl.ds(row0, PEER_NKEYS), ls] = jnp.exp(x0 - hs0[0:1])
            w1_ref[pl.ds(row0, PEER_NKEYS), ls] = jnp.exp(x1 - hs1[0:1]) / denom
        return carry

    lax.fori_loop(0, PEER_HEADS, head, 0)


def _router(hn, wq_t, sk):
    m, d = hn.shape
    nq = wq_t.shape[0]
    tr = 256 if m % 256 == 0 else LANES
    rows = PEER_HEADS * PEER_NKEYS
    tab = jax.ShapeDtypeStruct((rows, m), F32)
    tab_spec = pl.BlockSpec((rows, tr), lambda i: (0, i))
    return pl.pallas_call(
        _router_kernel,
        grid=(m // tr,),
        in_specs=[
            pl.BlockSpec((tr, d), lambda i: (i, 0)),
            pl.BlockSpec((nq, d), lambda i: (0, 0)),
            pl.BlockSpec((2, PEER_NKEYS, PEER_HALF), lambda i: (0, 0, 0)),
        ],
        out_specs=[tab_spec] * 4,
        out_shape=[tab] * 4,
        scratch_shapes=[pltpu.VMEM((nq, tr), F32)],
        compiler_params=_params("parallel"),
    )(hn, wq_t, sk)


def _peer_kernel(hn_ref, u_ref, vt_ref, n0_ref, r1_ref, w0_ref, w1_ref, o_ref, a_sc, act_sc):
    e = pl.program_id(1)
    te, tm = a_sc.shape
    rows_per_step = te // PEER_NKEYS

    @pl.when(e == 0)
    def _init():
        o_ref[...] = jnp.zeros(o_ref.shape, F32)

    a_sc[...] = lax.dot_general(u_ref[...], hn_ref[...], NT_DIMS, preferred_element_type=F32)
    steps_per_group = SUBLANES // rows_per_step
    group0 = pl.multiple_of((e // steps_per_group) * SUBLANES, SUBLANES)
    phase = e % steps_per_group

    def table_row(ref, h, ii, ls):
        tile = ref[pl.ds(h * PEER_NKEYS + group0, SUBLANES), ls]
        row = tile[ii:ii + 1]
        for s in range(1, steps_per_group):
            row = jnp.where(phase == s, tile[s * rows_per_step + ii:s * rows_per_step + ii + 1], row)
        return row

    for ii in range(rows_per_step):
        rs = slice(ii * PEER_NKEYS, (ii + 1) * PEER_NKEYS)
        for g in range(tm // LANES):
            ls = slice(g * LANES, (g + 1) * LANES)
            gate = jnp.zeros((PEER_NKEYS, LANES), F32)
            for h in range(PEER_HEADS):
                hs = slice(h * PEER_NKEYS, (h + 1) * PEER_NKEYS)
                n0 = table_row(n0_ref, h, ii, ls)
                w0 = table_row(w0_ref, h, ii, ls)
                gate = gate + jnp.where(r1_ref[hs, ls] < n0, w1_ref[hs, ls], 0.0) * w0
            a = a_sc[rs, ls]
            act = a * (lax.erf(a * INV_SQRT2) + 1.0) * 0.5 * gate
            act_sc[rs, ls] = act.astype(act_sc.dtype)
    o_ref[...] += jnp.dot(vt_ref[...], act_sc[...], preferred_element_type=F32)


def _peer(hn, u, vt, n0, r1, w0, w1):
    m, d = hn.shape
    ne = u.shape[0]
    tm = _row_tile(m)
    te = 512
    rows = PEER_HEADS * PEER_NKEYS
    tab_spec = pl.BlockSpec((rows, tm), lambda i, e: (0, i))
    return pl.pallas_call(
        _peer_kernel,
        grid=(m // tm, ne // te),
        in_specs=[
            pl.BlockSpec((tm, d), lambda i, e: (i, 0)),
            pl.BlockSpec((te, d), lambda i, e: (e, 0)),
            pl.BlockSpec((d, te), lambda i, e: (0, e)),
            tab_spec, tab_spec, tab_spec, tab_spec,
        ],
        out_specs=pl.BlockSpec((d, tm), lambda i, e: (0, i)),
        out_shape=jax.ShapeDtypeStruct((d, m), F32),
        scratch_shapes=[pltpu.VMEM((te, tm), F32), pltpu.VMEM((te, tm), BF16)],
        compiler_params=_params("parallel", "arbitrary"),
    )(hn, u, vt, n0, r1, w0, w1)


def _layer_weights(l, ln1, w_in, b_f, qn_a, kn_a, qn_b, kn_b, w_br_a, w_br_b, w_out, ln2, w_pq,
                   sub_keys, peer_u, peer_v, a_heads, b_heads):
    aw = a_heads * HEAD_DIM
    bw = b_heads * HEAD_DIM
    d = w_in.shape[1]
    w = w_in[l]
    o = np.cumsum([0, aw, aw, aw, bw, bw, bw, b_heads, d, d]).tolist()
    col = lambda i: w[:, o[i]:o[i + 1]]
    tile = lambda g, n: jnp.tile(g, n)
    f_pad = LANES - b_heads
    return dict(
        ln1=ln1[l], ln2=ln2[l],
        wq=jnp.concatenate([col(0), col(3)], axis=1).astype(BF16),
        wk=jnp.concatenate([col(1), col(4)], axis=1).astype(BF16),
        wv=jnp.concatenate([col(2), col(5)], axis=1).astype(BF16),
        wg=jnp.concatenate([col(7), col(8)], axis=1).astype(BF16),
        wf=jnp.pad(col(6), ((0, 0), (0, f_pad))).astype(BF16),
        gq=jnp.concatenate([tile(qn_a[l], a_heads), tile(qn_b[l], b_heads)]).reshape(1, aw + bw),
        gk=jnp.concatenate([tile(kn_a[l], a_heads), tile(kn_b[l], b_heads)]).reshape(1, aw + bw),
        bf=jnp.pad(b_f[l], (0, f_pad)).reshape(1, LANES),
        w_br_a=w_br_a[l].astype(BF16), w_br_b=w_br_b[l].astype(BF16), w_out=w_out[l].astype(BF16),
        wq_t=w_pq[l].T.astype(BF16), sk=sub_keys[l].astype(BF16),
        u=peer_u[l].astype(BF16), vt=peer_v[l].T.astype(BF16),
    )


def _in_proj(x2d, wt, a_heads, b_heads):
    aw = a_heads * HEAD_DIM
    bw = b_heads * HEAD_DIM
    xn = _rmsnorm_bf16(x2d, wt["ln1"])
    zeros = lambda n: jnp.zeros((1, n), F32)
    (q_bf,) = _proj(xn, wt["wq"], wt["gq"], "headnorm", [BF16])
    k_f, k_bf = _proj(xn, wt["wk"], wt["gk"], "headnorm", [F32, BF16])
    v_f, v_bf = _proj(xn, wt["wv"], zeros(aw + bw), "none", [F32, BF16])
    (gates,) = _proj(xn, wt["wg"], zeros(wt["wg"].shape[1]), "sigmoid", [F32])
    (lf_pad,) = _proj(xn, wt["wf"], wt["bf"], "logsigmoid", [F32])
    return q_bf, k_f, k_bf, v_f, v_bf, gates, lf_pad[:, :b_heads]


def _merge_ffn(x2d, ya, yb, gates, wt):
    mm = _merge(ya, yb, wt["w_br_a"], wt["w_br_b"], gates)
    h, hn = _outproj(mm, wt["w_out"], x2d, wt["ln2"])
    n0, r1, w0, w1 = _router(hn, wt["wq_t"], wt["sk"])
    peer_t = _peer(hn, wt["u"], wt["vt"], n0, r1, w0, w1)
    return h + peer_t.T


def _rel_index(qpos, kpos):
    return np.clip(qpos[:, None] - kpos[None, :], -REL_CLIP, REL_CLIP) + REL_CLIP


def kernel(x_prompt, x_sample, cache_a_k, cache_a_v, cache_b_k, cache_b_v, cache_b_lf, ln1, w_in, b_f, qn_a, kn_a, qn_b, kn_b, rel_bias, w_br_a, w_br_b, w_out, ln2, w_pq, sub_keys, peer_u, peer_v):
    bsz, seq, d = x_prompt.shape
    dbsz, dseq, _ = x_sample.shape
    depth = w_in.shape[0]
    a_heads = cache_a_k.shape[3]
    b_heads = cache_b_k.shape[3]
    aw = a_heads * HEAD_DIM
    bw = b_heads * HEAD_DIM
    past = cache_b_k.shape[2]
    la = cache_a_k.shape[2]
    keep = min(WINDOW, seq)

    rel_p = _rel_index(np.arange(CHUNK), np.arange(WINDOW + CHUNK) - WINDOW)
    rel_s = _rel_index(past + np.arange(dseq),
                       np.concatenate([past - la + np.arange(la), past + np.arange(dseq)]))
    lpad = -(past + dseq) % LANES

    hp = x_prompt.reshape(bsz * seq, d)
    hs = x_sample.reshape(dbsz * dseq, d)
    outs = [[] for _ in range(10)]
    for l in range(depth):
        wt = _layer_weights(l, ln1, w_in, b_f, qn_a, kn_a, qn_b, kn_b, w_br_a, w_br_b, w_out, ln2,
                            w_pq, sub_keys, peer_u, peer_v, a_heads, b_heads)
        bias_p = _relbias(rel_bias[l], rel_p)
        bias_s = _relbias(rel_bias[l], rel_s)

        q_bf, k_f, k_bf, v_f, v_bf, gates, lf = _in_proj(hp, wt, a_heads, b_heads)
        q3, k3, v3 = (z.reshape(bsz, seq, aw + bw) for z in (q_bf, k_bf, v_bf))
        ya = _band_prompt(q3, k3, v3, bias_p, a_heads)
        lf3 = lf.reshape(bsz, seq, b_heads)
        c = _cumsum_lanes(jnp.swapaxes(lf3, 1, 2))
        yb = _fox_prompt(q3, k3, v3, c.reshape(bsz * b_heads, seq, 1), c.reshape(bsz * b_heads, 1, seq),
                         b_heads, a_heads)
        k4 = k_f.reshape(bsz, seq, a_heads + b_heads, HEAD_DIM)
        v4 = v_f.reshape(bsz, seq, a_heads + b_heads, HEAD_DIM)
        outs[0].append(k4[:, seq - keep:, :a_heads])
        outs[1].append(v4[:, seq - keep:, :a_heads])
        outs[4].append(k4[:, :, a_heads:])
        outs[5].append(v4[:, :, a_heads:])
        outs[6].append(lf3)
        hp = _merge_ffn(hp, ya.reshape(bsz * seq, aw), yb.reshape(bsz * seq, bw), gates, wt)

        q_bf, k_f, k_bf, v_f, v_bf, gates, lf = _in_proj(hs, wt, a_heads, b_heads)
        q3, k3, v3 = (z.reshape(dbsz, dseq, aw + bw) for z in (q_bf, k_bf, v_bf))
        ya = _sample_attention(
            functools.partial(_band_sample_kernel, nheads=a_heads, head0=0), q3, k3, v3,
            cache_a_k[l].reshape(dbsz, la, aw), cache_a_v[l].reshape(dbsz, la, aw),
            [bias_s], [pl.BlockSpec(bias_s.shape, lambda i: (0, 0, 0))], a_heads)
        lf3 = lf.reshape(dbsz, dseq, b_heads)
        lf_all = jnp.concatenate([cache_b_lf[l].astype(F32), lf3], axis=1)
        c = _cumsum_lanes(jnp.pad(jnp.swapaxes(lf_all, 1, 2), ((0, 0), (0, 0), (0, lpad))))
        c_col = jnp.swapaxes(c[:, :, past:past + dseq], 1, 2)
        yb = _sample_attention(
            functools.partial(_fox_sample_kernel, nheads=b_heads, head0=a_heads), q3, k3, v3,
            cache_b_k[l].reshape(dbsz, past, bw), cache_b_v[l].reshape(dbsz, past, bw),
            [c_col, c],
            [pl.BlockSpec((1, dseq, b_heads), lambda i: (i, 0, 0)),
             pl.BlockSpec((1, b_heads, c.shape[2]), lambda i: (i, 0, 0))], b_heads)
        k4 = k_f.reshape(dbsz, dseq, a_heads + b_heads, HEAD_DIM)
        v4 = v_f.reshape(dbsz, dseq, a_heads + b_heads, HEAD_DIM)
        outs[2].append(k4[:, :, :a_heads])
        outs[3].append(v4[:, :, :a_heads])
        outs[7].append(k4[:, :, a_heads:])
        outs[8].append(v4[:, :, a_heads:])
        outs[9].append(lf3)
        hs = _merge_ffn(hs, ya.reshape(dbsz * dseq, aw), yb.reshape(dbsz * dseq, bw), gates, wt)

    stacked = [jnp.stack(o) for o in outs]
    return (hp.reshape(bsz, seq, d), hs.reshape(dbsz, dseq, d), *stacked)
```

```python
import functools

import numpy as np
import jax
import jax.numpy as jnp
from jax import lax
from jax.experimental import pallas as pl
from jax.experimental.pallas import tpu as pltpu

EPS = 1e-6
CHUNK = 64
HEAD_DIM = 128
LEFT_CHUNKS = 8
WINDOW = LEFT_CHUNKS * CHUNK
REL_CLIP = 128
PEER_HEADS = 8
PEER_NKEYS = 128
PEER_TOPK = 16
PEER_HALF = 128
ATT_SCALE = HEAD_DIM ** -0.5
NEG = -1e30
INV_SQRT2 = 0.7071067811865476
LANES = 128
SUBLANES = 8
MXU_DIM = 256
VMEM_LIMIT = 48 * 1024 * 1024

F32 = jnp.float32
BF16 = jnp.bfloat16
NT_DIMS = (((1,), (1,)), ((), ()))


def _params(*sem):
    return pltpu.CompilerParams(dimension_semantics=sem, vmem_limit_bytes=VMEM_LIMIT)


def _row_tile(m, pref=512):
    return pref if m % pref == 0 else m


def _pack_factor():
    return 4 // jnp.dtype(BF16).itemsize


def _pack_kernel(w_ref, o_ref, *, transpose):
    w = w_ref[...]
    if transpose:
        w = w.T
    o_ref[...] = pltpu.bitcast(w.astype(BF16), jnp.uint32)


def _pack_rows(w, transpose=False):
    r, c = w.shape
    pk = _pack_factor()
    tr = _row_tile(r)
    if transpose:
        out_shape, out_spec = (c // pk, r), pl.BlockSpec((c // pk, tr), lambda i: (0, i))
    else:
        out_shape, out_spec = (r // pk, c), pl.BlockSpec((tr // pk, c), lambda i: (i, 0))
    return pl.pallas_call(
        functools.partial(_pack_kernel, transpose=transpose),
        grid=(r // tr,),
        in_specs=[pl.BlockSpec((tr, c), lambda i: (i, 0))],
        out_specs=out_spec,
        out_shape=jax.ShapeDtypeStruct(out_shape, jnp.uint32),
        compiler_params=_params("parallel"),
        name="pack_rows_t" if transpose else "pack_rows",
    )(w)


def _rmsnorm_kernel(x_ref, g_ref, o_ref):
    x = x_ref[...]
    ms = jnp.mean(x * x, axis=-1, keepdims=True)
    o_ref[...] = (x * lax.rsqrt(ms + EPS) * g_ref[...]).astype(o_ref.dtype)


def _rmsnorm_bf16(x2d, g):
    m, d = x2d.shape
    tm = _row_tile(m)
    return pl.pallas_call(
        _rmsnorm_kernel,
        grid=(m // tm,),
        in_specs=[pl.BlockSpec((tm, d), lambda i: (i, 0)), pl.BlockSpec((1, d), lambda i: (0, 0))],
        out_specs=pl.BlockSpec((tm, d), lambda i: (i, 0)),
        out_shape=jax.ShapeDtypeStruct((m, d), BF16),
        compiler_params=_params("parallel"),
        name="rmsnorm",
    )(x2d, g.reshape(1, d))


def _proj_kernel(x_ref, w_ref, aux_ref, *out_refs, epilogue):
    acc = jnp.dot(x_ref[...], w_ref[...], preferred_element_type=F32)
    tn = acc.shape[1]
    if epilogue == "headnorm":
        for c in range(tn // HEAD_DIM):
            sl = slice(c * HEAD_DIM, (c + 1) * HEAD_DIM)
            y = acc[:, sl]
            ms = jnp.mean(y * y, axis=-1, keepdims=True)
            r = y * lax.rsqrt(ms + EPS) * aux_ref[:, sl]
            for o in out_refs:
                o[:, sl] = r.astype(o.dtype)
        return
    if epilogue == "sigmoid":
        res = jax.nn.sigmoid(acc)
    elif epilogue == "logsigmoid":
        z = acc + aux_ref[...]
        res = jnp.minimum(z, 0.0) - jnp.log1p(jnp.exp(-jnp.abs(z)))
    else:
        res = acc
    for o in out_refs:
        o[...] = res.astype(o.dtype)


def _proj(xn, w, aux, epilogue, out_dtypes):
    m, k = xn.shape
    n = w.shape[1]
    tm = _row_tile(m)
    tn = min(512, n)
    outs = pl.pallas_call(
        functools.partial(_proj_kernel, epilogue=epilogue),
        grid=(m // tm, n // tn),
        in_specs=[
            pl.BlockSpec((tm, k), lambda i, j: (i, 0)),
            pl.BlockSpec((k, tn), lambda i, j: (0, j)),
            pl.BlockSpec((1, tn), lambda i, j: (0, j)),
        ],
        out_specs=[pl.BlockSpec((tm, tn), lambda i, j: (i, j)) for _ in out_dtypes],
        out_shape=[jax.ShapeDtypeStruct((m, n), dt) for dt in out_dtypes],
        compiler_params=_params("parallel", "arbitrary"),
        name="proj_" + epilogue,
    )(xn, w, aux)
    return outs


def _cumsum_kernel(x_ref, o_ref):
    x = x_ref[0]
    n = x.shape[1]
    lane = lax.broadcasted_iota(jnp.int32, x.shape, 1)
    s = 1
    while s < n:
        x = x + jnp.where(lane >= s, pltpu.roll(x, s, 1), 0.0)
        s *= 2
    o_ref[0] = x


def _cumsum_lanes(x):
    b, h, n = x.shape
    return pl.pallas_call(
        _cumsum_kernel,
        grid=(b,),
        in_specs=[pl.BlockSpec((1, h, n), lambda i: (i, 0, 0))],
        out_specs=pl.BlockSpec((1, h, n), lambda i: (i, 0, 0)),
        out_shape=jax.ShapeDtypeStruct((b, h, n), F32),
        compiler_params=_params("parallel"),
        name="cumsum",
    )(x)


def _relbias_kernel(tab_ref, rel_ref, o_ref):
    rel = rel_ref[...]
    nh, nr = tab_ref.shape
    for h in range(nh):
        def body(r, acc, h=h):
            return jnp.where(rel == r, tab_ref[h, r], acc)
        o_ref[h] = lax.fori_loop(0, nr, body, jnp.zeros(rel.shape, F32))


def _relbias(table, rel):
    nh = table.shape[0]
    q, k = rel.shape
    return pl.pallas_call(
        _relbias_kernel,
        in_specs=[pl.BlockSpec(memory_space=pltpu.SMEM), pl.BlockSpec((q, k), lambda: (0, 0))],
        out_specs=pl.BlockSpec((nh, q, k), lambda: (0, 0, 0)),
        out_shape=jax.ShapeDtypeStruct((nh, q, k), F32),
        name="relbias",
    )(table, jnp.asarray(rel, jnp.int32))


def _band_prompt_kernel(q_ref, k_ref, v_ref, bias_ref, o_ref, kpad, vpad):
    s_len = q_ref.shape[1]
    kw_len = WINDOW + CHUNK
    kpad[0:WINDOW, :] = jnp.zeros((WINDOW, HEAD_DIM), kpad.dtype)
    vpad[0:WINDOW, :] = jnp.zeros((WINDOW, HEAD_DIM), vpad.dtype)
    kpad[WINDOW:, :] = k_ref[0]
    vpad[WINDOW:, :] = v_ref[0]
    bias = bias_ref[0]
    kk = lax.broadcasted_iota(jnp.int32, (CHUNK, kw_len), 1)

    def body(n, carry):
        start = pl.multiple_of(n * CHUNK, CHUNK)
        q = q_ref[0, pl.ds(start, CHUNK), :]
        kw = kpad[pl.ds(start, kw_len), :]
        vw = vpad[pl.ds(start, kw_len), :]
        s = lax.dot_general(q, kw, NT_DIMS, preferred_element_type=F32) * ATT_SCALE + bias
        s = jnp.where(kk >= WINDOW - start, s, NEG)
        m = jnp.max(s, axis=-1, keepdims=True)
        p = jnp.exp(s - m)
        l = jnp.sum(p, axis=-1, keepdims=True)
        o = jnp.dot(p.astype(BF16), vw, preferred_element_type=F32) / l
        o_ref[0, pl.ds(start, CHUNK), :] = o.astype(o_ref.dtype)
        return carry

    lax.fori_loop(0, s_len // CHUNK, body, 0)


def _band_prompt(q, k, v, bias, nheads):
    b, s, _ = q.shape
    blk = pl.BlockSpec((1, s, HEAD_DIM), lambda i, h: (i, 0, h))
    return pl.pallas_call(
        _band_prompt_kernel,
        grid=(b, nheads),
        in_specs=[blk, blk, blk,
                  pl.BlockSpec((1, CHUNK, WINDOW + CHUNK), lambda i, h: (h, 0, 0))],
        out_specs=blk,
        out_shape=jax.ShapeDtypeStruct((b, s, nheads * HEAD_DIM), BF16),
        scratch_shapes=[pltpu.VMEM((WINDOW + s, HEAD_DIM), BF16),
                        pltpu.VMEM((WINDOW + s, HEAD_DIM), BF16)],
        compiler_params=_params("parallel", "parallel"),
        name="band_prompt",
    )(q, k, v, bias)


def _fox_prompt_kernel(q_ref, k_ref, v_ref, cq_ref, ck_ref, o_ref, m_sc, l_sc, acc_sc):
    qi = pl.program_id(2)
    ki = pl.program_id(3)
    tq = q_ref.shape[1]
    tk = k_ref.shape[1]

    @pl.when(ki == 0)
    def _init():
        m_sc[...] = jnp.full(m_sc.shape, NEG, F32)
        l_sc[...] = jnp.zeros(l_sc.shape, F32)
        acc_sc[...] = jnp.zeros(acc_sc.shape, F32)

    def step(diagonal):
        s = lax.dot_general(q_ref[0], k_ref[0], NT_DIMS, preferred_element_type=F32) * ATT_SCALE
        s = s + cq_ref[0] - ck_ref[0]
        if diagonal:
            row = lax.broadcasted_iota(jnp.int32, (tq, tk), 0)
            col = lax.broadcasted_iota(jnp.int32, (tq, tk), 1)
            s = jnp.where(col <= row, s, NEG)
        m_prev = m_sc[...]
        m_new = jnp.maximum(m_prev, jnp.max(s, axis=-1, keepdims=True))
        alpha = jnp.exp(m_prev - m_new)
        p = jnp.exp(s - m_new)
        l_sc[...] = alpha * l_sc[...] + jnp.sum(p, axis=-1, keepdims=True)
        acc_sc[...] = alpha * acc_sc[...] + jnp.dot(p.astype(BF16), v_ref[0], preferred_element_type=F32)
        m_sc[...] = m_new

    @pl.when(ki < qi)
    def _below():
        step(False)

    @pl.when(ki == qi)
    def _diag():
        step(True)
        o_ref[0] = (acc_sc[...] / l_sc[...]).astype(o_ref.dtype)


def _fox_prompt(q, k, v, c_col, c_row, nheads, head0):
    b, s, _ = q.shape
    t = min(512, s)
    nq = s // t
    return pl.pallas_call(
        _fox_prompt_kernel,
        grid=(b, nheads, nq, nq),
        in_specs=[
            pl.BlockSpec((1, t, HEAD_DIM), lambda i, h, qi, ki: (i, qi, head0 + h)),
            pl.BlockSpec((1, t, HEAD_DIM), lambda i, h, qi, ki: (i, jnp.minimum(ki, qi), head0 + h)),
            pl.BlockSpec((1, t, HEAD_DIM), lambda i, h, qi, ki: (i, jnp.minimum(ki, qi), head0 + h)),
            pl.BlockSpec((1, t, 1), lambda i, h, qi, ki: (i * nheads + h, qi, 0)),
            pl.BlockSpec((1, 1, t), lambda i, h, qi, ki: (i * nheads + h, 0, jnp.minimum(ki, qi))),
        ],
        out_specs=pl.BlockSpec((1, t, HEAD_DIM), lambda i, h, qi, ki: (i, qi, h)),
        out_shape=jax.ShapeDtypeStruct((b, s, nheads * HEAD_DIM), BF16),
        scratch_shapes=[pltpu.VMEM((t, 1), F32), pltpu.VMEM((t, 1), F32), pltpu.VMEM((t, HEAD_DIM), F32)],
        compiler_params=_params("parallel", "parallel", "parallel", "arbitrary"),
        name="fox_prompt",
    )(q, k, v, c_col, c_row)


def _two_part_attention(qh, kc, vc, kn, vn, add_c, add_n, causal_new):
    sc = lax.dot_general(qh, kc, NT_DIMS, preferred_element_type=F32) * ATT_SCALE + add_c
    sn = lax.dot_general(qh, kn, NT_DIMS, preferred_element_type=F32) * ATT_SCALE + add_n
    if causal_new:
        row = lax.broadcasted_iota(jnp.int32, sn.shape, 0)
        col = lax.broadcasted_iota(jnp.int32, sn.shape, 1)
        sn = jnp.where(col <= row, sn, NEG)
    m = jnp.maximum(jnp.max(sc, axis=-1, keepdims=True), jnp.max(sn, axis=-1, keepdims=True))
    pc = jnp.exp(sc - m)
    pn = jnp.exp(sn - m)
    l = jnp.sum(pc, axis=-1, keepdims=True) + jnp.sum(pn, axis=-1, keepdims=True)
    o = (jnp.dot(pc.astype(BF16), vc, preferred_element_type=F32)
         + jnp.dot(pn.astype(BF16), vn, preferred_element_type=F32))
    return o / l


def _band_sample_kernel(q_ref, kn_ref, vn_ref, kc_ref, vc_ref, bias_ref, o_ref, *, nheads, head0):
    lc = kc_ref.shape[1]
    for h in range(nheads):
        sl = slice((head0 + h) * HEAD_DIM, (head0 + h + 1) * HEAD_DIM)
        cl = slice(h * HEAD_DIM, (h + 1) * HEAD_DIM)
        bias = bias_ref[h]
        o = _two_part_attention(
            q_ref[0, :, sl], kc_ref[0, :, cl].astype(BF16), vc_ref[0, :, cl].astype(BF16),
            kn_ref[0, :, sl], vn_ref[0, :, sl], bias[:, :lc], bias[:, lc:], causal_new=False)
        o_ref[0, :, cl] = o.astype(o_ref.dtype)


def _fox_sample_kernel(q_ref, kn_ref, vn_ref, kc_ref, vc_ref, ccol_ref, crow_ref, o_ref, *, nheads, head0):
    lc = kc_ref.shape[1]
    t = q_ref.shape[1]
    for h in range(nheads):
        sl = slice((head0 + h) * HEAD_DIM, (head0 + h + 1) * HEAD_DIM)
        cl = slice(h * HEAD_DIM, (h + 1) * HEAD_DIM)
        cq = ccol_ref[0, :, h:h + 1]
        o = _two_part_attention(
            q_ref[0, :, sl], kc_ref[0, :, cl].astype(BF16), vc_ref[0, :, cl].astype(BF16),
            kn_ref[0, :, sl], vn_ref[0, :, sl],
            cq - crow_ref[0, h:h + 1, 0:lc], cq - crow_ref[0, h:h + 1, lc:lc + t], causal_new=True)
        o_ref[0, :, cl] = o.astype(o_ref.dtype)


def _sample_attention(kernel_fn, q, kn, vn, kc, vc, extras, extra_specs, nheads):
    b, t, w = q.shape
    lc = kc.shape[1]
    new_blk = pl.BlockSpec((1, t, w), lambda i: (i, 0, 0))
    cache_blk = pl.BlockSpec((1, lc, nheads * HEAD_DIM), lambda i: (i, 0, 0))
    return pl.pallas_call(
        kernel_fn,
        grid=(b,),
        in_specs=[new_blk, new_blk, new_blk, cache_blk, cache_blk] + extra_specs,
        out_specs=pl.BlockSpec((1, t, nheads * HEAD_DIM), lambda i: (i, 0, 0)),
        out_shape=jax.ShapeDtypeStruct((b, t, nheads * HEAD_DIM), BF16),
        compiler_params=_params("parallel"),
        name="sample_attention",
    )(q, kn, vn, kc, vc, *extras)


def _merge_kernel(ya_ref, yb_ref, wa_ref, wb_ref, ga_ref, gb_ref, o_ref):
    a = jnp.dot(ya_ref[...], wa_ref[...], preferred_element_type=F32)
    b = jnp.dot(yb_ref[...], wb_ref[...], preferred_element_type=F32)
    o_ref[...] = (ga_ref[...] * a + gb_ref[...] * b).astype(o_ref.dtype)


def _merge(ya, yb, wa, wb, gates):
    m, ka = ya.shape
    n = wa.shape[1]
    tm = _row_tile(m)
    tn = min(512, n)
    nj = n // tn
    return pl.pallas_call(
        _merge_kernel,
        grid=(m // tm, nj),
        in_specs=[
            pl.BlockSpec((tm, ka), lambda i, j: (i, 0)),
            pl.BlockSpec((tm, yb.shape[1]), lambda i, j: (i, 0)),
            pl.BlockSpec((ka, tn), lambda i, j: (0, j)),
            pl.BlockSpec((wb.shape[0], tn), lambda i, j: (0, j)),
            pl.BlockSpec((tm, tn), lambda i, j: (i, j)),
            pl.BlockSpec((tm, tn), lambda i, j: (i, j + nj)),
        ],
        out_specs=pl.BlockSpec((tm, tn), lambda i, j: (i, j)),
        out_shape=jax.ShapeDtypeStruct((m, n), BF16),
        compiler_params=_params("parallel", "arbitrary"),
        name="merge",
    )(ya, yb, wa, wb, gates, gates)


def _outproj_kernel(m_ref, w_ref, x_ref, g_ref, h_ref, hn_ref):
    h = x_ref[...] + jnp.dot(m_ref[...], w_ref[...], preferred_element_type=F32)
    h_ref[...] = h
    ms = jnp.mean(h * h, axis=-1, keepdims=True)
    hn_ref[...] = (h * lax.rsqrt(ms + EPS) * g_ref[...]).T.astype(hn_ref.dtype)


def _outproj(mm, w_out, x2d, ln2):
    m, k = mm.shape
    d = w_out.shape[1]
    tm = _row_tile(m, 256)
    return pl.pallas_call(
        _outproj_kernel,
        grid=(m // tm,),
        in_specs=[
            pl.BlockSpec((tm, k), lambda i: (i, 0)),
            pl.BlockSpec((k, d), lambda i: (0, 0)),
            pl.BlockSpec((tm, d), lambda i: (i, 0)),
            pl.BlockSpec((1, d), lambda i: (0, 0)),
        ],
        out_specs=[pl.BlockSpec((tm, d), lambda i: (i, 0)), pl.BlockSpec((d, tm), lambda i: (0, i))],
        out_shape=[jax.ShapeDtypeStruct((m, d), F32), jax.ShapeDtypeStruct((d, m), BF16)],
        compiler_params=_params("parallel"),
        name="outproj",
    )(mm, w_out, x2d, ln2.reshape(1, d))


def _extract_topk(x):
    n = x.shape[0]
    iota = lax.broadcasted_iota(jnp.int32, x.shape, 0).astype(F32)
    rank = jnp.full(x.shape, float(PEER_TOPK), F32)
    vals = []
    for r in range(PEER_TOPK):
        m = jnp.max(x, axis=0, keepdims=True)
        first = jnp.min(jnp.where(x == m, iota, float(n)), axis=0, keepdims=True)
        hit = iota == first
        rank = jnp.where(hit, float(r), rank)
        x = jnp.where(hit, -jnp.inf, x)
        vals.append(m)
    return rank, jnp.concatenate(vals, axis=0)


def _merge_counts(hs0, hs1):
    k = hs0.shape[0]
    a_iota = lax.broadcasted_iota(jnp.int32, hs0.shape, 0).astype(F32)
    count = jnp.zeros(hs0.shape, F32)
    front = hs0 + hs1[0:1]
    best = None
    denom = None
    for r in range(k):
        m = jnp.max(front, axis=0, keepdims=True)
        if r == 0:
            best = m
            denom = jnp.ones_like(m)
        else:
            denom = denom + jnp.exp(m - best)
        first = jnp.min(jnp.where(front == m, a_iota, float(k)), axis=0, keepdims=True)
        hit = a_iota == first
        count = jnp.where(hit, count + 1.0, count)
        nxt = jnp.full(hs0.shape, -jnp.inf, F32)
        for b in range(1, k):
            nxt = jnp.where(count == float(b), hs1[b:b + 1], nxt)
        front = jnp.where(hit, hs0 + nxt, front)
    return count, best, denom


def _router_kernel(hnt_ref, wq_ref, sk_ref, n0_ref, r1_ref, w0_ref, w1_ref, q_sc):
    tr = hnt_ref.shape[1]
    q_sc[...] = jnp.dot(pltpu.bitcast(wq_ref[...], BF16), hnt_ref[...], preferred_element_type=F32)

    def head(h, carry):
        base = pl.multiple_of(h * (2 * PEER_HALF), 2 * PEER_HALF)
        row0 = pl.multiple_of(h * PEER_NKEYS, PEER_NKEYS)
        s0 = jnp.dot(sk_ref[0], q_sc[pl.ds(base, PEER_HALF), :].astype(BF16), preferred_element_type=F32)
        s1 = jnp.dot(sk_ref[1], q_sc[pl.ds(base + PEER_HALF, PEER_HALF), :].astype(BF16),
                     preferred_element_type=F32)
        for g in range(tr // LANES):
            ls = slice(g * LANES, (g + 1) * LANES)
            x0 = s0[:, ls]
            x1 = s1[:, ls]
            rank0, hs0 = _extract_topk(x0)
            rank1, hs1 = _extract_topk(x1)
            count, _, denom = _merge_counts(hs0, hs1)
            n0 = jnp.zeros(x0.shape, F32)
            for a in range(PEER_TOPK):
                n0 = jnp.where(rank0 == float(a), count[a:a + 1], n0)
            n0_ref[pl.ds(row0, PEER_NKEYS), ls] = n0
            r1_ref[pl.ds(row0, PEER_NKEYS), ls] = rank1.astype(r1_ref.dtype)
            w0_ref[pl.ds(row0, PEER_NKEYS), ls] = jnp.exp(x0 - hs0[0:1])
            w1_ref[pl.ds(row0, PEER_NKEYS), ls] = (jnp.exp(x1 - hs1[0:1]) / denom).astype(w1_ref.dtype)
        return carry

    lax.fori_loop(0, PEER_HEADS, head, 0)


def _router(hnt, wq_t, sk):
    d, m = hnt.shape
    nq = wq_t.shape[0] * _pack_factor()
    tr = 256 if m % 256 == 0 else LANES
    rows = PEER_HEADS * PEER_NKEYS
    tab_spec = pl.BlockSpec((rows, tr), lambda i: (0, i))
    return pl.pallas_call(
        _router_kernel,
        grid=(m // tr,),
        in_specs=[
            pl.BlockSpec((d, tr), lambda i: (0, i)),
            pl.BlockSpec(wq_t.shape, lambda i: (0, 0)),
            pl.BlockSpec((2, PEER_NKEYS, PEER_HALF), lambda i: (0, 0, 0)),
        ],
        out_specs=[tab_spec] * 4,
        out_shape=[jax.ShapeDtypeStruct((rows, m), dt) for dt in (F32, BF16, F32, BF16)],
        scratch_shapes=[pltpu.VMEM((nq, tr), F32)],
        compiler_params=_params("parallel"),
        name="peer_router",
    )(hnt, wq_t, sk)


def _peer_kernel(hnt_ref, u_ref, vt_ref, n0_ref, r1_ref, w0_ref, w1_ref, o_ref, gate_sc, act_sc):
    e = pl.program_id(1)
    ne = pl.num_programs(1)
    _, te, tm = gate_sc.shape
    d = o_ref.shape[0]
    rows_per_step = te // PEER_NKEYS
    steps_per_group = SUBLANES // rows_per_step
    gw = min(tm, 2 * LANES)
    blk = (PEER_NKEYS, gw)
    gdt = r1_ref.dtype
    pk = _pack_factor()
    blocks = [(ii, g) for ii in range(rows_per_step) for g in range(tm // gw)]

    def build_gate(step, slot, block_list):
        group0 = pl.multiple_of((step // steps_per_group) * SUBLANES, SUBLANES)
        phase = step % steps_per_group

        def table_row(ref, h, ii, ls):
            tile = ref[pl.ds(h * PEER_NKEYS + group0, SUBLANES), ls]
            row = tile[ii:ii + 1]
            for s in range(1, steps_per_group):
                row = jnp.where(phase == s, tile[s * rows_per_step + ii:s * rows_per_step + ii + 1], row)
            x = jnp.broadcast_to(row, (2 * SUBLANES, gw)).astype(gdt)
            return jnp.broadcast_to(x[None], (PEER_NKEYS // (2 * SUBLANES), 2 * SUBLANES, gw)).reshape(blk)

        for ii, g in block_list:
            rs = slice(ii * PEER_NKEYS, (ii + 1) * PEER_NKEYS)
            ls = slice(g * gw, (g + 1) * gw)
            gate = jnp.zeros(blk, gdt)
            for h in range(PEER_HEADS):
                hs = slice(h * PEER_NKEYS, (h + 1) * PEER_NKEYS)
                n0 = table_row(n0_ref, h, ii, ls)
                w0 = table_row(w0_ref, h, ii, ls)
                gate = gate + jnp.where(r1_ref[hs, ls] < n0, w1_ref[hs, ls], jnp.zeros(blk, gdt)) * w0
            gate_sc[slot, rs, ls] = gate.astype(F32)

    @pl.when(e == 0)
    def _init():
        o_ref[...] = jnp.zeros(o_ref.shape, F32)
        build_gate(e, 0, blocks)

    cur = e % 2
    nxt = 1 - cur
    e_next = jnp.minimum(e + 1, ne - 1)
    n_early = len(blocks) // 2

    for ii in range(rows_per_step):
        rs = slice(ii * PEER_NKEYS, (ii + 1) * PEER_NKEYS)
        u_rows = pltpu.bitcast(u_ref[ii * PEER_NKEYS // pk:(ii + 1) * PEER_NKEYS // pk, :], BF16)
        a = jnp.dot(u_rows, hnt_ref[...], preferred_element_type=F32)
        build_gate(e_next, nxt, blocks[ii * n_early // rows_per_step:(ii + 1) * n_early // rows_per_step])
        act_sc[rs, :] = (a * (lax.erf(a * INV_SQRT2) + 1.0) * 0.5 * gate_sc[cur, rs, :]).astype(BF16)
    n_out = d // MXU_DIM
    late = blocks[n_early:]
    for dr in range(n_out):
        dsl = slice(dr * MXU_DIM, (dr + 1) * MXU_DIM)
        vt_rows = pltpu.bitcast(vt_ref[dr * MXU_DIM // pk:(dr + 1) * MXU_DIM // pk, :], BF16)
        o_ref[dsl, :] += jnp.dot(vt_rows, act_sc[...], preferred_element_type=F32)
        build_gate(e_next, nxt, late[dr * len(late) // n_out:(dr + 1) * len(late) // n_out])


def _peer(hnt, u, vt, n0, r1, w0, w1):
    d, m = hnt.shape
    ne = vt.shape[1]
    pk = _pack_factor()
    tm = _row_tile(m)
    te = 512
    rows = PEER_HEADS * PEER_NKEYS
    tab_spec = pl.BlockSpec((rows, tm), lambda i, e: (0, i))
    return pl.pallas_call(
        _peer_kernel,
        grid=(m // tm, ne // te),
        in_specs=[
            pl.BlockSpec((d, tm), lambda i, e: (0, i)),
            pl.BlockSpec((te // pk, d), lambda i, e: (e, 0)),
            pl.BlockSpec((d // pk, te), lambda i, e: (0, e)),
            tab_spec, tab_spec, tab_spec, tab_spec,
        ],
        out_specs=pl.BlockSpec((d, tm), lambda i, e: (0, i)),
        out_shape=jax.ShapeDtypeStruct((d, m), F32),
        scratch_shapes=[pltpu.VMEM((2, te, tm), F32), pltpu.VMEM((te, tm), BF16)],
        compiler_params=_params("parallel", "arbitrary"),
        name="peer_mixer",
    )(hnt, u, vt, n0, r1, w0, w1)


def _layer_weights(l, ln1, w_in, b_f, qn_a, kn_a, qn_b, kn_b, w_br_a, w_br_b, w_out, ln2, w_pq,
                   sub_keys, peer_u, peer_v, a_heads, b_heads):
    aw = a_heads * HEAD_DIM
    bw = b_heads * HEAD_DIM
    d = w_in.shape[1]
    w = w_in[l]
    o = np.cumsum([0, aw, aw, aw, bw, bw, bw, b_heads, d, d]).tolist()
    col = lambda i: w[:, o[i]:o[i + 1]]
    tile = lambda g, n: jnp.tile(g, n)
    f_pad = LANES - b_heads
    return dict(
        ln1=ln1[l], ln2=ln2[l],
        wq=jnp.concatenate([col(0), col(3)], axis=1).astype(BF16),
        wk=jnp.concatenate([col(1), col(4)], axis=1).astype(BF16),
        wv=jnp.concatenate([col(2), col(5)], axis=1).astype(BF16),
        wg=jnp.concatenate([col(7), col(8)], axis=1).astype(BF16),
        wf=jnp.pad(col(6), ((0, 0), (0, f_pad))).astype(BF16),
        gq=jnp.concatenate([tile(qn_a[l], a_heads), tile(qn_b[l], b_heads)]).reshape(1, aw + bw),
        gk=jnp.concatenate([tile(kn_a[l], a_heads), tile(kn_b[l], b_heads)]).reshape(1, aw + bw),
        bf=jnp.pad(b_f[l], (0, f_pad)).reshape(1, LANES),
        w_br_a=w_br_a[l].astype(BF16), w_br_b=w_br_b[l].astype(BF16), w_out=w_out[l].astype(BF16),
        wq_t=_pack_rows(w_pq[l], transpose=True), sk=sub_keys[l].astype(BF16),
        u=_pack_rows(peer_u[l]), vt=_pack_rows(peer_v[l], transpose=True),
    )


def _in_proj(x2d, wt, a_heads, b_heads):
    aw = a_heads * HEAD_DIM
    bw = b_heads * HEAD_DIM
    xn = _rmsnorm_bf16(x2d, wt["ln1"])
    zeros = lambda n: jnp.zeros((1, n), F32)
    (q_bf,) = _proj(xn, wt["wq"], wt["gq"], "headnorm", [BF16])
    k_f, k_bf = _proj(xn, wt["wk"], wt["gk"], "headnorm", [F32, BF16])
    v_f, v_bf = _proj(xn, wt["wv"], zeros(aw + bw), "none", [F32, BF16])
    (gates,) = _proj(xn, wt["wg"], zeros(wt["wg"].shape[1]), "sigmoid", [F32])
    (lf_pad,) = _proj(xn, wt["wf"], wt["bf"], "logsigmoid", [F32])
    return q_bf, k_f, k_bf, v_f, v_bf, gates, lf_pad[:, :b_heads]


def _merge_ffn(x2d, ya, yb, gates, wt):
    mm = _merge(ya, yb, wt["w_br_a"], wt["w_br_b"], gates)
    h, hnt = _outproj(mm, wt["w_out"], x2d, wt["ln2"])
    n0, r1, w0, w1 = _router(hnt, wt["wq_t"], wt["sk"])
    peer_t = _peer(hnt, wt["u"], wt["vt"], n0, r1, w0, w1)
    return h + peer_t.T


def _rel_index(qpos, kpos):
    return np.clip(qpos[:, None] - kpos[None, :], -REL_CLIP, REL_CLIP) + REL_CLIP


def kernel(x_prompt, x_sample, cache_a_k, cache_a_v, cache_b_k, cache_b_v, cache_b_lf, ln1, w_in, b_f, qn_a, kn_a, qn_b, kn_b, rel_bias, w_br_a, w_br_b, w_out, ln2, w_pq, sub_keys, peer_u, peer_v):
    bsz, seq, d = x_prompt.shape
    dbsz, dseq, _ = x_sample.shape
    depth = w_in.shape[0]
    a_heads = cache_a_k.shape[3]
    b_heads = cache_b_k.shape[3]
    aw = a_heads * HEAD_DIM
    bw = b_heads * HEAD_DIM
    past = cache_b_k.shape[2]
    la = cache_a_k.shape[2]
    keep = min(WINDOW, seq)

    rel_p = _rel_index(np.arange(CHUNK), np.arange(WINDOW + CHUNK) - WINDOW)
    rel_s = _rel_index(past + np.arange(dseq),
                       np.concatenate([past - la + np.arange(la), past + np.arange(dseq)]))
    lpad = -(past + dseq) % LANES

    hp = x_prompt.reshape(bsz * seq, d)
    hs = x_sample.reshape(dbsz * dseq, d)
    outs = [[] for _ in range(10)]
    for l in range(depth):
        wt = _layer_weights(l, ln1, w_in, b_f, qn_a, kn_a, qn_b, kn_b, w_br_a, w_br_b, w_out, ln2,
                            w_pq, sub_keys, peer_u, peer_v, a_heads, b_heads)
        bias_p = _relbias(rel_bias[l], rel_p)
        bias_s = _relbias(rel_bias[l], rel_s)

        q_bf, k_f, k_bf, v_f, v_bf, gates, lf = _in_proj(hp, wt, a_heads, b_heads)
        q3, k3, v3 = (z.reshape(bsz, seq, aw + bw) for z in (q_bf, k_bf, v_bf))
        ya = _band_prompt(q3, k3, v3, bias_p, a_heads)
        lf3 = lf.reshape(bsz, seq, b_heads)
        c = _cumsum_lanes(jnp.swapaxes(lf3, 1, 2))
        yb = _fox_prompt(q3, k3, v3, c.reshape(bsz * b_heads, seq, 1), c.reshape(bsz * b_heads, 1, seq),
                         b_heads, a_heads)
        k4 = k_f.reshape(bsz, seq, a_heads + b_heads, HEAD_DIM)
        v4 = v_f.reshape(bsz, seq, a_heads + b_heads, HEAD_DIM)
        outs[0].append(k4[:, seq - keep:, :a_heads])
        outs[1].append(v4[:, seq - keep:, :a_heads])
        outs[4].append(k4[:, :, a_heads:])
        outs[5].append(v4[:, :, a_heads:])
        outs[6].append(lf3)
        hp = _merge_ffn(hp, ya.reshape(bsz * seq, aw), yb.reshape(bsz * seq, bw), gates, wt)

        q_bf, k_f, k_bf, v_f, v_bf, gates, lf = _in_proj(hs, wt, a_heads, b_heads)
        q3, k3, v3 = (z.reshape(dbsz, dseq, aw + bw) for z in (q_bf, k_bf, v_bf))
        ya = _sample_attention(
            functools.partial(_band_sample_kernel, nheads=a_heads, head0=0), q3, k3, v3,
            cache_a_k[l].reshape(dbsz, la, aw), cache_a_v[l].reshape(dbsz, la, aw),
            [bias_s], [pl.BlockSpec(bias_s.shape, lambda i: (0, 0, 0))], a_heads)
        lf3 = lf.reshape(dbsz, dseq, b_heads)
        lf_all = jnp.concatenate([cache_b_lf[l].astype(F32), lf3], axis=1)
        c = _cumsum_lanes(jnp.pad(jnp.swapaxes(lf_all, 1, 2), ((0, 0), (0, 0), (0, lpad))))
        c_col = jnp.swapaxes(c[:, :, past:past + dseq], 1, 2)
        yb = _sample_attention(
            functools.partial(_fox_sample_kernel, nheads=b_heads, head0=a_heads), q3, k3, v3,
            cache_b_k[l].reshape(dbsz, past, bw), cache_b_v[l].reshape(dbsz, past, bw),
            [c_col, c],
            [pl.BlockSpec((1, dseq, b_heads), lambda i: (i, 0, 0)),
             pl.BlockSpec((1, b_heads, c.shape[2]), lambda i: (i, 0, 0))], b_heads)
        k4 = k_f.reshape(dbsz, dseq, a_heads + b_heads, HEAD_DIM)
        v4 = v_f.reshape(dbsz, dseq, a_heads + b_heads, HEAD_DIM)
        outs[2].append(k4[:, :, :a_heads])
        outs[3].append(v4[:, :, :a_heads])
        outs[7].append(k4[:, :, a_heads:])
        outs[8].append(v4[:, :, a_heads:])
        outs[9].append(lf3)
        hs = _merge_ffn(hs, ya.reshape(dbsz * dseq, aw), yb.reshape(dbsz * dseq, bw), gates, wt)

    stacked = [jnp.stack(o) for o in outs]
    return (hp.reshape(bsz, seq, d), hs.reshape(dbsz, dseq, d), *stacked)
```

```python
import functools

import numpy as np
import jax
import jax.numpy as jnp
from jax import lax
from jax.experimental import pallas as pl
from jax.experimental.pallas import tpu as pltpu

EPS = 1e-6
CHUNK = 64
HEAD_DIM = 128
LEFT_CHUNKS = 8
WINDOW = LEFT_CHUNKS * CHUNK
REL_CLIP = 128
PEER_HEADS = 8
PEER_NKEYS = 128
PEER_TOPK = 16
PEER_HALF = 128
ATT_SCALE = HEAD_DIM ** -0.5
NEG = -1e30
INV_SQRT2 = 0.7071067811865476
LANES = 128
SUBLANES = 8
MXU_DIM = 256
FOX_TQ = 512
FOX_TK = 256
VMEM_LIMIT = 48 * 1024 * 1024

F32 = jnp.float32
BF16 = jnp.bfloat16
NT_DIMS = (((1,), (1,)), ((), ()))
TN_DIMS = (((0,), (0,)), ((), ()))


def _params(*sem):
    return pltpu.CompilerParams(dimension_semantics=sem, vmem_limit_bytes=VMEM_LIMIT)


def _row_tile(m, pref=512):
    return pref if m % pref == 0 else m


def _pack_factor():
    return 4 // jnp.dtype(BF16).itemsize


def _pack_kernel(w_ref, o_ref, *, transpose):
    w = w_ref[...]
    if transpose:
        w = w.T
    o_ref[...] = pltpu.bitcast(w.astype(BF16), jnp.uint32)


def _pack_rows(w, transpose=False):
    r, c = w.shape
    pk = _pack_factor()
    tr = _row_tile(r)
    if transpose:
        out_shape, out_spec = (c // pk, r), pl.BlockSpec((c // pk, tr), lambda i: (0, i))
    else:
        out_shape, out_spec = (r // pk, c), pl.BlockSpec((tr // pk, c), lambda i: (i, 0))
    return pl.pallas_call(
        functools.partial(_pack_kernel, transpose=transpose),
        grid=(r // tr,),
        in_specs=[pl.BlockSpec((tr, c), lambda i: (i, 0))],
        out_specs=out_spec,
        out_shape=jax.ShapeDtypeStruct(out_shape, jnp.uint32),
        compiler_params=_params("parallel"),
        name="pack_rows_t" if transpose else "pack_rows",
    )(w)


def _rmsnorm_kernel(x_ref, g_ref, o_ref):
    x = x_ref[...]
    ms = jnp.mean(x * x, axis=-1, keepdims=True)
    o_ref[...] = (x * lax.rsqrt(ms + EPS) * g_ref[...]).astype(o_ref.dtype)


def _rmsnorm_bf16(x2d, g):
    m, d = x2d.shape
    tm = _row_tile(m)
    return pl.pallas_call(
        _rmsnorm_kernel,
        grid=(m // tm,),
        in_specs=[pl.BlockSpec((tm, d), lambda i: (i, 0)), pl.BlockSpec((1, d), lambda i: (0, 0))],
        out_specs=pl.BlockSpec((tm, d), lambda i: (i, 0)),
        out_shape=jax.ShapeDtypeStruct((m, d), BF16),
        compiler_params=_params("parallel"),
        name="rmsnorm",
    )(x2d, g.reshape(1, d))


def _proj_kernel(x_ref, w_ref, aux_ref, *out_refs, epilogue):
    acc = jnp.dot(x_ref[...], w_ref[...], preferred_element_type=F32)
    tn = acc.shape[1]
    if epilogue == "headnorm":
        for c in range(tn // HEAD_DIM):
            sl = slice(c * HEAD_DIM, (c + 1) * HEAD_DIM)
            y = acc[:, sl]
            ms = jnp.mean(y * y, axis=-1, keepdims=True)
            r = y * lax.rsqrt(ms + EPS) * aux_ref[:, sl]
            for o in out_refs:
                o[:, sl] = r.astype(o.dtype)
        return
    if epilogue == "sigmoid":
        res = jax.nn.sigmoid(acc)
    elif epilogue == "logsigmoid":
        z = acc + aux_ref[...]
        res = jnp.minimum(z, 0.0) - jnp.log1p(jnp.exp(-jnp.abs(z)))
    else:
        res = acc
    for o in out_refs:
        o[...] = res.astype(o.dtype)


def _proj(xn, w, aux, epilogue, out_dtypes):
    m, k = xn.shape
    n = w.shape[1]
    tm = _row_tile(m)
    tn = min(512, n)
    outs = pl.pallas_call(
        functools.partial(_proj_kernel, epilogue=epilogue),
        grid=(m // tm, n // tn),
        in_specs=[
            pl.BlockSpec((tm, k), lambda i, j: (i, 0)),
            pl.BlockSpec((k, tn), lambda i, j: (0, j)),
            pl.BlockSpec((1, tn), lambda i, j: (0, j)),
        ],
        out_specs=[pl.BlockSpec((tm, tn), lambda i, j: (i, j)) for _ in out_dtypes],
        out_shape=[jax.ShapeDtypeStruct((m, n), dt) for dt in out_dtypes],
        compiler_params=_params("parallel", "arbitrary"),
        name="proj_" + epilogue,
    )(xn, w, aux)
    return outs


def _cumsum_kernel(x_ref, o_ref):
    x = x_ref[0]
    n = x.shape[1]
    lane = lax.broadcasted_iota(jnp.int32, x.shape, 1)
    s = 1
    while s < n:
        x = x + jnp.where(lane >= s, pltpu.roll(x, s, 1), 0.0)
        s *= 2
    o_ref[0] = x


def _cumsum_lanes(x):
    b, h, n = x.shape
    return pl.pallas_call(
        _cumsum_kernel,
        grid=(b,),
        in_specs=[pl.BlockSpec((1, h, n), lambda i: (i, 0, 0))],
        out_specs=pl.BlockSpec((1, h, n), lambda i: (i, 0, 0)),
        out_shape=jax.ShapeDtypeStruct((b, h, n), F32),
        compiler_params=_params("parallel"),
        name="cumsum",
    )(x)


def _relbias_kernel(tab_ref, rel_ref, valid_ref, o_ref, *, fill, r0, r1):
    rel = rel_ref[r0:r1, :]
    valid = valid_ref[...] != 0
    nh, nr = tab_ref.shape
    for h in range(nh):
        o_ref[h] = jnp.full(o_ref.shape[1:], tab_ref[h, fill], F32)

        def body(r, acc, h=h):
            return jnp.where(rel == r, tab_ref[h, r], acc)
        o_ref[h, r0:r1, :] = lax.fori_loop(0, nr, body, jnp.zeros(rel.shape, F32))
        o_ref[h] = jnp.where(valid, o_ref[h], NEG)


def _relbias(table, rel, valid=None):
    nh = table.shape[0]
    rel = np.asarray(rel, np.int32)
    valid = np.ones(rel.shape, np.int32) if valid is None else np.asarray(valid, np.int32)
    fill = int(np.bincount(rel.ravel()).argmax())
    rows = np.nonzero((rel != fill).any(axis=1))[0]
    r0 = int(rows.min()) // SUBLANES * SUBLANES
    r1 = -(-(int(rows.max()) + 1) // SUBLANES) * SUBLANES
    i, j = rel.shape
    full = pl.BlockSpec((i, j), lambda: (0, 0))
    return pl.pallas_call(
        functools.partial(_relbias_kernel, fill=fill, r0=r0, r1=min(r1, i)),
        in_specs=[pl.BlockSpec(memory_space=pltpu.SMEM), full, full],
        out_specs=pl.BlockSpec((nh, i, j), lambda: (0, 0, 0)),
        out_shape=jax.ShapeDtypeStruct((nh, i, j), F32),
        name="relbias",
    )(table, jnp.asarray(rel), jnp.asarray(valid))


BAND_TILE = 2 * CHUNK
BAND_KEYS = WINDOW + BAND_TILE


def _band_prompt_kernel(q_ref, k_ref, v_ref, bias_ref, o_ref, kpad, vpad):
    s_len = q_ref.shape[1]
    kpad[0:WINDOW, :] = jnp.zeros((WINDOW, HEAD_DIM), kpad.dtype)
    vpad[0:WINDOW, :] = jnp.zeros((WINDOW, HEAD_DIM), vpad.dtype)
    kpad[WINDOW:, :] = k_ref[0]
    vpad[WINDOW:, :] = v_ref[0]
    bias = bias_ref[0]
    kk = lax.broadcasted_iota(jnp.int32, (BAND_KEYS, BAND_TILE), 0)

    def body(n, carry):
        start = pl.multiple_of(n * BAND_TILE, BAND_TILE)
        q = q_ref[0, pl.ds(start, BAND_TILE), :]
        kw = kpad[pl.ds(start, BAND_KEYS), :]
        vw = vpad[pl.ds(start, BAND_KEYS), :]
        s = lax.dot_general(kw, q, NT_DIMS, preferred_element_type=F32) * ATT_SCALE + bias
        s = jnp.where(kk >= WINDOW - start, s, NEG)
        m = jnp.max(s, axis=0, keepdims=True)
        p = jnp.exp(s - m)
        l = jnp.sum(p, axis=0, keepdims=True)
        o = lax.dot_general(vw, p.astype(BF16), TN_DIMS, preferred_element_type=F32) / l
        o_ref[0, pl.ds(start, BAND_TILE), :] = o.T.astype(o_ref.dtype)
        return carry

    lax.fori_loop(0, s_len // BAND_TILE, body, 0, unroll=True)


def _band_prompt(q, k, v, bias, nheads):
    b, s, _ = q.shape
    blk = pl.BlockSpec((1, s, HEAD_DIM), lambda i, h: (i, 0, h))
    return pl.pallas_call(
        _band_prompt_kernel,
        grid=(b, nheads),
        in_specs=[blk, blk, blk,
                  pl.BlockSpec((1, BAND_KEYS, BAND_TILE), lambda i, h: (h, 0, 0))],
        out_specs=blk,
        out_shape=jax.ShapeDtypeStruct((b, s, nheads * HEAD_DIM), BF16),
        scratch_shapes=[pltpu.VMEM((WINDOW + s, HEAD_DIM), BF16),
                        pltpu.VMEM((WINDOW + s, HEAD_DIM), BF16)],
        compiler_params=_params("parallel", "parallel"),
        name="band_prompt",
    )(q, k, v, bias)


def _fox_prompt_kernel(q_ref, k_ref, v_ref, cq_ref, ck_ref, o_ref, m_sc, l_sc, acc_sc, *, nheads, tk):
    qi = pl.program_id(1)
    tq = q_ref.shape[1]
    m_sc[...] = jnp.full(m_sc.shape, NEG, F32)
    l_sc[...] = jnp.zeros(l_sc.shape, F32)
    acc_sc[...] = jnp.zeros(acc_sc.shape, F32)

    def tile(kstart, masked):
        for h in range(nheads):
            cl = slice(h * HEAD_DIM, (h + 1) * HEAD_DIM)
            k = k_ref[0, pl.ds(kstart, tk), cl]
            v = v_ref[0, pl.ds(kstart, tk), cl]
            s = lax.dot_general(k, q_ref[0, :, cl], NT_DIMS, preferred_element_type=F32) * ATT_SCALE
            s = s + cq_ref[0, h:h + 1, :] - ck_ref[0, pl.ds(kstart, tk), h:h + 1]
            if masked:
                key = kstart + lax.broadcasted_iota(jnp.int32, (tk, tq), 0)
                qry = qi * tq + lax.broadcasted_iota(jnp.int32, (tk, tq), 1)
                s = jnp.where(key <= qry, s, NEG)
            m_prev = m_sc[h:h + 1, :]
            m_new = jnp.maximum(m_prev, jnp.max(s, axis=0, keepdims=True))
            alpha = jnp.exp(m_prev - m_new)
            p = jnp.exp(s - m_new)
            l_sc[h:h + 1, :] = alpha * l_sc[h:h + 1, :] + jnp.sum(p, axis=0, keepdims=True)
            pv = lax.dot_general(v, p.astype(BF16), TN_DIMS, preferred_element_type=F32)
            acc_sc[cl, :] = alpha * acc_sc[cl, :] + pv
            m_sc[h:h + 1, :] = m_new

    per_q = tq // tk

    def below(kb, carry):
        tile(pl.multiple_of(kb * tk, tk), False)
        return carry

    lax.fori_loop(0, qi * per_q, below, 0)
    for j in range(per_q):
        tile(pl.multiple_of(qi * tq + j * tk, tk), True)
    for h in range(nheads):
        cl = slice(h * HEAD_DIM, (h + 1) * HEAD_DIM)
        o_ref[0, :, cl] = (acc_sc[cl, :] / l_sc[h:h + 1, :]).T.astype(o_ref.dtype)


def _fox_prompt(q, k, v, c_row, c_col, nheads, head0):
    b, s, _ = q.shape
    tq = min(FOX_TQ, s)
    tk = min(FOX_TK, s)
    hw = nheads * HEAD_DIM
    cb = head0 // nheads
    return pl.pallas_call(
        functools.partial(_fox_prompt_kernel, nheads=nheads, tk=tk),
        grid=(b, s // tq),
        in_specs=[
            pl.BlockSpec((1, tq, hw), lambda i, qi: (i, qi, cb)),
            pl.BlockSpec((1, s, hw), lambda i, qi: (i, 0, cb)),
            pl.BlockSpec((1, s, hw), lambda i, qi: (i, 0, cb)),
            pl.BlockSpec((1, nheads, tq), lambda i, qi: (i, 0, qi)),
            pl.BlockSpec((1, s, nheads), lambda i, qi: (i, 0, 0)),
        ],
        out_specs=pl.BlockSpec((1, tq, hw), lambda i, qi: (i, qi, 0)),
        out_shape=jax.ShapeDtypeStruct((b, s, hw), BF16),
        scratch_shapes=[pltpu.VMEM((nheads, tq), F32), pltpu.VMEM((nheads, tq), F32),
                        pltpu.VMEM((hw, tq), F32)],
        compiler_params=_params("parallel", "arbitrary"),
        name="fox_prompt",
    )(q, k, v, c_row, c_col)


def _two_part_attention(qh, kc, vc, kn, vn, add_c, add_n, causal_new):
    sc = lax.dot_general(qh, kc, NT_DIMS, preferred_element_type=F32) * ATT_SCALE + add_c
    sn = lax.dot_general(qh, kn, NT_DIMS, preferred_element_type=F32) * ATT_SCALE + add_n
    if causal_new:
        row = lax.broadcasted_iota(jnp.int32, sn.shape, 0)
        col = lax.broadcasted_iota(jnp.int32, sn.shape, 1)
        sn = jnp.where(col <= row, sn, NEG)
    m = jnp.maximum(jnp.max(sc, axis=-1, keepdims=True), jnp.max(sn, axis=-1, keepdims=True))
    pc = jnp.exp(sc - m)
    pn = jnp.exp(sn - m)
    l = jnp.sum(pc, axis=-1, keepdims=True) + jnp.sum(pn, axis=-1, keepdims=True)
    o = (jnp.dot(pc.astype(BF16), vc, preferred_element_type=F32)
         + jnp.dot(pn.astype(BF16), vn, preferred_element_type=F32))
    return o / l


def _band_sample_kernel(q_ref, kn_ref, vn_ref, kc_ref, vc_ref, bias_ref, o_ref, *, nheads, head0):
    lc = kc_ref.shape[1]
    for h in range(nheads):
        sl = slice((head0 + h) * HEAD_DIM, (head0 + h + 1) * HEAD_DIM)
        cl = slice(h * HEAD_DIM, (h + 1) * HEAD_DIM)
        bias = bias_ref[h]
        o = _two_part_attention(
            q_ref[0, :, sl], kc_ref[0, :, cl].astype(BF16), vc_ref[0, :, cl].astype(BF16),
            kn_ref[0, :, sl], vn_ref[0, :, sl], bias[:, :lc], bias[:, lc:], causal_new=False)
        o_ref[0, :, cl] = o.astype(o_ref.dtype)


def _fox_sample_kernel(q_ref, kn_ref, vn_ref, kc_ref, vc_ref, ccol_ref, crow_ref, o_ref, *, nheads, head0):
    lc = kc_ref.shape[1]
    t = q_ref.shape[1]
    for h in range(nheads):
        sl = slice((head0 + h) * HEAD_DIM, (head0 + h + 1) * HEAD_DIM)
        cl = slice(h * HEAD_DIM, (h + 1) * HEAD_DIM)
        cq = ccol_ref[0, :, h:h + 1]
        o = _two_part_attention(
            q_ref[0, :, sl], kc_ref[0, :, cl].astype(BF16), vc_ref[0, :, cl].astype(BF16),
            kn_ref[0, :, sl], vn_ref[0, :, sl],
            cq - crow_ref[0, h:h + 1, 0:lc], cq - crow_ref[0, h:h + 1, lc:lc + t], causal_new=True)
        o_ref[0, :, cl] = o.astype(o_ref.dtype)


def _sample_attention(kernel_fn, q, kn, vn, kc, vc, extras, extra_specs, nheads):
    b, t, w = q.shape
    lc = kc.shape[1]
    new_blk = pl.BlockSpec((1, t, w), lambda i: (i, 0, 0))
    cache_blk = pl.BlockSpec((1, lc, nheads * HEAD_DIM), lambda i: (i, 0, 0))
    return pl.pallas_call(
        kernel_fn,
        grid=(b,),
        in_specs=[new_blk, new_blk, new_blk, cache_blk, cache_blk] + extra_specs,
        out_specs=pl.BlockSpec((1, t, nheads * HEAD_DIM), lambda i: (i, 0, 0)),
        out_shape=jax.ShapeDtypeStruct((b, t, nheads * HEAD_DIM), BF16),
        compiler_params=_params("parallel"),
        name="sample_attention",
    )(q, kn, vn, kc, vc, *extras)


def _merge_kernel(ya_ref, yb_ref, wa_ref, wb_ref, ga_ref, gb_ref, o_ref):
    a = jnp.dot(ya_ref[...], wa_ref[...], preferred_element_type=F32)
    b = jnp.dot(yb_ref[...], wb_ref[...], preferred_element_type=F32)
    o_ref[...] = (ga_ref[...] * a + gb_ref[...] * b).astype(o_ref.dtype)


def _merge(ya, yb, wa, wb, gates):
    m, ka = ya.shape
    n = wa.shape[1]
    tm = _row_tile(m)
    tn = min(512, n)
    nj = n // tn
    return pl.pallas_call(
        _merge_kernel,
        grid=(m // tm, nj),
        in_specs=[
            pl.BlockSpec((tm, ka), lambda i, j: (i, 0)),
            pl.BlockSpec((tm, yb.shape[1]), lambda i, j: (i, 0)),
            pl.BlockSpec((ka, tn), lambda i, j: (0, j)),
            pl.BlockSpec((wb.shape[0], tn), lambda i, j: (0, j)),
            pl.BlockSpec((tm, tn), lambda i, j: (i, j)),
            pl.BlockSpec((tm, tn), lambda i, j: (i, j + nj)),
        ],
        out_specs=pl.BlockSpec((tm, tn), lambda i, j: (i, j)),
        out_shape=jax.ShapeDtypeStruct((m, n), BF16),
        compiler_params=_params("parallel", "arbitrary"),
        name="merge",
    )(ya, yb, wa, wb, gates, gates)


def _outproj_kernel(m_ref, w_ref, x_ref, g_ref, h_ref, hn_ref):
    h = x_ref[...] + jnp.dot(m_ref[...], w_ref[...], preferred_element_type=F32)
    h_ref[...] = h
    ms = jnp.mean(h * h, axis=-1, keepdims=True)
    hn_ref[...] = (h * lax.rsqrt(ms + EPS) * g_ref[...]).T.astype(hn_ref.dtype)


def _outproj(mm, w_out, x2d, ln2):
    m, k = mm.shape
    d = w_out.shape[1]
    tm = _row_tile(m, 256)
    return pl.pallas_call(
        _outproj_kernel,
        grid=(m // tm,),
        in_specs=[
            pl.BlockSpec((tm, k), lambda i: (i, 0)),
            pl.BlockSpec((k, d), lambda i: (0, 0)),
            pl.BlockSpec((tm, d), lambda i: (i, 0)),
            pl.BlockSpec((1, d), lambda i: (0, 0)),
        ],
        out_specs=[pl.BlockSpec((tm, d), lambda i: (i, 0)), pl.BlockSpec((d, tm), lambda i: (0, i))],
        out_shape=[jax.ShapeDtypeStruct((m, d), F32), jax.ShapeDtypeStruct((d, m), BF16)],
        compiler_params=_params("parallel"),
        name="outproj",
    )(mm, w_out, x2d, ln2.reshape(1, d))


def _extract_topk(x):
    n = x.shape[0]
    iota = lax.broadcasted_iota(jnp.int32, x.shape, 0).astype(F32)
    rank = jnp.full(x.shape, float(PEER_TOPK), F32)
    vals = []
    for r in range(PEER_TOPK):
        m = jnp.max(x, axis=0, keepdims=True)
        first = jnp.min(jnp.where(x == m, iota, float(n)), axis=0, keepdims=True)
        hit = iota == first
        rank = jnp.where(hit, float(r), rank)
        x = jnp.where(hit, -jnp.inf, x)
        vals.append(m)
    return rank, jnp.concatenate(vals, axis=0)


def _merge_counts(hs0, hs1):
    k = hs0.shape[0]
    a_iota = lax.broadcasted_iota(jnp.int32, hs0.shape, 0).astype(F32)
    count = jnp.zeros(hs0.shape, F32)
    front = hs0 + hs1[0:1]
    best = None
    denom = None
    for r in range(k):
        m = jnp.max(front, axis=0, keepdims=True)
        if r == 0:
            best = m
            denom = jnp.ones_like(m)
        else:
            denom = denom + jnp.exp(m - best)
        first = jnp.min(jnp.where(front == m, a_iota, float(k)), axis=0, keepdims=True)
        hit = a_iota == first
        count = jnp.where(hit, count + 1.0, count)
        nxt = jnp.full(hs0.shape, -jnp.inf, F32)
        for b in range(1, k):
            nxt = jnp.where(count == float(b), hs1[b:b + 1], nxt)
        front = jnp.where(hit, hs0 + nxt, front)
    return count, best, denom


def _router_kernel(hnt_ref, wq_ref, sk_ref, n0_ref, r1_ref, w0_ref, w1_ref, q_sc):
    tr = hnt_ref.shape[1]
    q_sc[...] = jnp.dot(pltpu.bitcast(wq_ref[...], BF16), hnt_ref[...], preferred_element_type=F32)

    def head(h, carry):
        base = pl.multiple_of(h * (2 * PEER_HALF), 2 * PEER_HALF)
        row0 = pl.multiple_of(h * PEER_NKEYS, PEER_NKEYS)
        s0 = jnp.dot(sk_ref[0], q_sc[pl.ds(base, PEER_HALF), :].astype(BF16), preferred_element_type=F32)
        s1 = jnp.dot(sk_ref[1], q_sc[pl.ds(base + PEER_HALF, PEER_HALF), :].astype(BF16),
                     preferred_element_type=F32)
        for g in range(tr // LANES):
            ls = slice(g * LANES, (g + 1) * LANES)
            x0 = s0[:, ls]
            x1 = s1[:, ls]
            rank0, hs0 = _extract_topk(x0)
            rank1, hs1 = _extract_topk(x1)
            count, _, denom = _merge_counts(hs0, hs1)
            n0 = jnp.zeros(x0.shape, F32)
            for a in range(PEER_TOPK):
                n0 = jnp.where(rank0 == float(a), count[a:a + 1], n0)
            n0_ref[pl.ds(row0, PEER_NKEYS), ls] = n0
            r1_ref[pl.ds(row0, PEER_NKEYS), ls] = rank1.astype(r1_ref.dtype)
            w0_ref[pl.ds(row0, PEER_NKEYS), ls] = jnp.exp(x0 - hs0[0:1])
            w1_ref[pl.ds(row0, PEER_NKEYS), ls] = (jnp.exp(x1 - hs1[0:1]) / denom).astype(w1_ref.dtype)
        return carry

    lax.fori_loop(0, PEER_HEADS, head, 0)


def _router(hnt, wq_t, sk):
    d, m = hnt.shape
    nq = wq_t.shape[0] * _pack_factor()
    tr = 256 if m % 256 == 0 else LANES
    rows = PEER_HEADS * PEER_NKEYS
    tab_spec = pl.BlockSpec((rows, tr), lambda i: (0, i))
    return pl.pallas_call(
        _router_kernel,
        grid=(m // tr,),
        in_specs=[
            pl.BlockSpec((d, tr), lambda i: (0, i)),
            pl.BlockSpec(wq_t.shape, lambda i: (0, 0)),
            pl.BlockSpec((2, PEER_NKEYS, PEER_HALF), lambda i: (0, 0, 0)),
        ],
        out_specs=[tab_spec] * 4,
        out_shape=[jax.ShapeDtypeStruct((rows, m), dt) for dt in (F32, BF16, F32, BF16)],
        scratch_shapes=[pltpu.VMEM((nq, tr), F32)],
        compiler_params=_params("parallel"),
        name="peer_router",
    )(hnt, wq_t, sk)


def _peer_kernel(hnt_ref, u_ref, vt_ref, n0_ref, r1_ref, w0_ref, w1_ref, o_ref, gate_sc, act_sc):
    e = pl.program_id(1)
    ne = pl.num_programs(1)
    _, te, tm = gate_sc.shape
    d = o_ref.shape[0]
    rows_per_step = te // PEER_NKEYS
    steps_per_group = SUBLANES // rows_per_step
    gw = min(tm, 2 * LANES)
    blk = (PEER_NKEYS, gw)
    gdt = r1_ref.dtype
    pk = _pack_factor()
    blocks = [(ii, g) for ii in range(rows_per_step) for g in range(tm // gw)]

    def build_gate(step, slot, block_list):
        group0 = pl.multiple_of((step // steps_per_group) * SUBLANES, SUBLANES)
        phase = step % steps_per_group

        def table_row(ref, h, ii, ls):
            tile = ref[pl.ds(h * PEER_NKEYS + group0, SUBLANES), ls]
            row = tile[ii:ii + 1]
            for s in range(1, steps_per_group):
                row = jnp.where(phase == s, tile[s * rows_per_step + ii:s * rows_per_step + ii + 1], row)
            x = jnp.broadcast_to(row, (2 * SUBLANES, gw)).astype(gdt)
            return jnp.broadcast_to(x[None], (PEER_NKEYS // (2 * SUBLANES), 2 * SUBLANES, gw)).reshape(blk)

        for ii, g in block_list:
            rs = slice(ii * PEER_NKEYS, (ii + 1) * PEER_NKEYS)
            ls = slice(g * gw, (g + 1) * gw)
            gate = jnp.zeros(blk, gdt)
            for h in range(PEER_HEADS):
                hs = slice(h * PEER_NKEYS, (h + 1) * PEER_NKEYS)
                n0 = table_row(n0_ref, h, ii, ls)
                w0 = table_row(w0_ref, h, ii, ls)
                gate = gate + jnp.where(r1_ref[hs, ls] < n0, w1_ref[hs, ls], jnp.zeros(blk, gdt)) * w0
            gate_sc[slot, rs, ls] = gate.astype(F32)

    @pl.when(e == 0)
    def _init():
        o_ref[...] = jnp.zeros(o_ref.shape, F32)
        build_gate(e, 0, blocks)

    cur = e % 2
    nxt = 1 - cur
    e_next = jnp.minimum(e + 1, ne - 1)
    n_early = len(blocks) // 2

    tw = min(tm, MXU_DIM)
    pieces = [(p, t) for p in range(te // MXU_DIM) for t in range(tm // tw)]
    for n, (p, t) in enumerate(pieces):
        rs = slice(p * MXU_DIM, (p + 1) * MXU_DIM)
        ts = slice(t * tw, (t + 1) * tw)
        u_rows = pltpu.bitcast(u_ref[p * MXU_DIM // pk:(p + 1) * MXU_DIM // pk, :], BF16)
        a = jnp.dot(u_rows, hnt_ref[:, ts], preferred_element_type=F32)
        build_gate(e_next, nxt, blocks[n * n_early // len(pieces):(n + 1) * n_early // len(pieces)])
        act_sc[rs, ts] = (a * (lax.erf(a * INV_SQRT2) + 1.0) * 0.5 * gate_sc[cur, rs, ts]).astype(BF16)
    orow = min(d, 2 * MXU_DIM)
    opieces = [(r, t) for r in range(d // orow) for t in range(tm // tw)]
    late = blocks[n_early:]
    for n, (r, t) in enumerate(opieces):
        dsl = slice(r * orow, (r + 1) * orow)
        ts = slice(t * tw, (t + 1) * tw)
        vt_rows = pltpu.bitcast(vt_ref[r * orow // pk:(r + 1) * orow // pk, :], BF16)
        o_ref[dsl, ts] += jnp.dot(vt_rows, act_sc[:, ts], preferred_element_type=F32)
        build_gate(e_next, nxt, late[n * len(late) // len(opieces):(n + 1) * len(late) // len(opieces)])


def _peer(hnt, u, vt, n0, r1, w0, w1):
    d, m = hnt.shape
    ne = vt.shape[1]
    pk = _pack_factor()
    tm = _row_tile(m)
    te = 512
    rows = PEER_HEADS * PEER_NKEYS
    tab_spec = pl.BlockSpec((rows, tm), lambda i, e: (0, i))
    return pl.pallas_call(
        _peer_kernel,
        grid=(m // tm, ne // te),
        in_specs=[
            pl.BlockSpec((d, tm), lambda i, e: (0, i)),
            pl.BlockSpec((te // pk, d), lambda i, e: (e, 0)),
            pl.BlockSpec((d // pk, te), lambda i, e: (0, e)),
            tab_spec, tab_spec, tab_spec, tab_spec,
        ],
        out_specs=pl.BlockSpec((d, tm), lambda i, e: (0, i)),
        out_shape=jax.ShapeDtypeStruct((d, m), F32),
        scratch_shapes=[pltpu.VMEM((2, te, tm), F32), pltpu.VMEM((te, tm), BF16)],
        compiler_params=_params("parallel", "arbitrary"),
        name="peer_mixer",
    )(hnt, u, vt, n0, r1, w0, w1)


def _layer_weights(l, ln1, w_in, b_f, qn_a, kn_a, qn_b, kn_b, w_br_a, w_br_b, w_out, ln2, w_pq,
                   sub_keys, peer_u, peer_v, a_heads, b_heads):
    aw = a_heads * HEAD_DIM
    bw = b_heads * HEAD_DIM
    d = w_in.shape[1]
    w = w_in[l]
    o = np.cumsum([0, aw, aw, aw, bw, bw, bw, b_heads, d, d]).tolist()
    col = lambda i: w[:, o[i]:o[i + 1]]
    tile = lambda g, n: jnp.tile(g, n)
    f_pad = LANES - b_heads
    return dict(
        ln1=ln1[l], ln2=ln2[l],
        wq=jnp.concatenate([col(0), col(3)], axis=1).astype(BF16),
        wk=jnp.concatenate([col(1), col(4)], axis=1).astype(BF16),
        wv=jnp.concatenate([col(2), col(5)], axis=1).astype(BF16),
        wg=jnp.concatenate([col(7), col(8)], axis=1).astype(BF16),
        wf=jnp.pad(col(6), ((0, 0), (0, f_pad))).astype(BF16),
        gq=jnp.concatenate([tile(qn_a[l], a_heads), tile(qn_b[l], b_heads)]).reshape(1, aw + bw),
        gk=jnp.concatenate([tile(kn_a[l], a_heads), tile(kn_b[l], b_heads)]).reshape(1, aw + bw),
        bf=jnp.pad(b_f[l], (0, f_pad)).reshape(1, LANES),
        w_br_a=w_br_a[l].astype(BF16), w_br_b=w_br_b[l].astype(BF16), w_out=w_out[l].astype(BF16),
        wq_t=_pack_rows(w_pq[l], transpose=True), sk=sub_keys[l].astype(BF16),
        u=_pack_rows(peer_u[l]), vt=_pack_rows(peer_v[l], transpose=True),
    )


def _in_proj(x2d, wt, a_heads, b_heads):
    aw = a_heads * HEAD_DIM
    bw = b_heads * HEAD_DIM
    xn = _rmsnorm_bf16(x2d, wt["ln1"])
    zeros = lambda n: jnp.zeros((1, n), F32)
    (q_bf,) = _proj(xn, wt["wq"], wt["gq"], "headnorm", [BF16])
    k_f, k_bf = _proj(xn, wt["wk"], wt["gk"], "headnorm", [F32, BF16])
    v_f, v_bf = _proj(xn, wt["wv"], zeros(aw + bw), "none", [F32, BF16])
    (gates,) = _proj(xn, wt["wg"], zeros(wt["wg"].shape[1]), "sigmoid", [BF16])
    (lf_pad,) = _proj(xn, wt["wf"], wt["bf"], "logsigmoid", [F32])
    return q_bf, k_f, k_bf, v_f, v_bf, gates, lf_pad[:, :b_heads]


def _merge_ffn(x2d, ya, yb, gates, wt):
    mm = _merge(ya, yb, wt["w_br_a"], wt["w_br_b"], gates)
    h, hnt = _outproj(mm, wt["w_out"], x2d, wt["ln2"])
    n0, r1, w0, w1 = _router(hnt, wt["wq_t"], wt["sk"])
    peer_t = _peer(hnt, wt["u"], wt["vt"], n0, r1, w0, w1)
    return h + peer_t.T


def _rel_index(qpos, kpos):
    return np.clip(qpos[:, None] - kpos[None, :], -REL_CLIP, REL_CLIP) + REL_CLIP


def kernel(x_prompt, x_sample, cache_a_k, cache_a_v, cache_b_k, cache_b_v, cache_b_lf, ln1, w_in, b_f, qn_a, kn_a, qn_b, kn_b, rel_bias, w_br_a, w_br_b, w_out, ln2, w_pq, sub_keys, peer_u, peer_v):
    bsz, seq, d = x_prompt.shape
    dbsz, dseq, _ = x_sample.shape
    depth = w_in.shape[0]
    a_heads = cache_a_k.shape[3]
    b_heads = cache_b_k.shape[3]
    aw = a_heads * HEAD_DIM
    bw = b_heads * HEAD_DIM
    past = cache_b_k.shape[2]
    la = cache_a_k.shape[2]
    keep = min(WINDOW, seq)

    kk, qq = np.arange(BAND_KEYS)[:, None], np.arange(BAND_TILE)[None, :]
    in_window = (kk >= qq // CHUNK * CHUNK) & (kk < qq // CHUNK * CHUNK + WINDOW + CHUNK)
    rel_p = _rel_index(np.arange(BAND_TILE), np.arange(BAND_KEYS) - WINDOW).T
    rel_s = _rel_index(past + np.arange(dseq),
                       np.concatenate([past - la + np.arange(la), past + np.arange(dseq)]))
    lpad = -(past + dseq) % LANES

    hp = x_prompt.reshape(bsz * seq, d)
    hs = x_sample.reshape(dbsz * dseq, d)
    outs = [[] for _ in range(10)]
    for l in range(depth):
        wt = _layer_weights(l, ln1, w_in, b_f, qn_a, kn_a, qn_b, kn_b, w_br_a, w_br_b, w_out, ln2,
                            w_pq, sub_keys, peer_u, peer_v, a_heads, b_heads)
        bias_p = _relbias(rel_bias[l], rel_p, in_window)
        bias_s = _relbias(rel_bias[l], rel_s)

        q_bf, k_f, k_bf, v_f, v_bf, gates, lf = _in_proj(hp, wt, a_heads, b_heads)
        q3, k3, v3 = (z.reshape(bsz, seq, aw + bw) for z in (q_bf, k_bf, v_bf))
        ya = _band_prompt(q3, k3, v3, bias_p, a_heads)
        lf3 = lf.reshape(bsz, seq, b_heads)
        c = _cumsum_lanes(jnp.swapaxes(lf3, 1, 2))
        yb = _fox_prompt(q3, k3, v3, c, jnp.swapaxes(c, 1, 2), b_heads, a_heads)
        k4 = k_f.reshape(bsz, seq, a_heads + b_heads, HEAD_DIM)
        v4 = v_f.reshape(bsz, seq, a_heads + b_heads, HEAD_DIM)
        outs[0].append(k4[:, seq - keep:, :a_heads])
        outs[1].append(v4[:, seq - keep:, :a_heads])
        outs[4].append(k4[:, :, a_heads:])
        outs[5].append(v4[:, :, a_heads:])
        outs[6].append(lf3)
        hp = _merge_ffn(hp, ya.reshape(bsz * seq, aw), yb.reshape(bsz * seq, bw), gates, wt)

        q_bf, k_f, k_bf, v_f, v_bf, gates, lf = _in_proj(hs, wt, a_heads, b_heads)
        q3, k3, v3 = (z.reshape(dbsz, dseq, aw + bw) for z in (q_bf, k_bf, v_bf))
        ya = _sample_attention(
            functools.partial(_band_sample_kernel, nheads=a_heads, head0=0), q3, k3, v3,
            cache_a_k[l].reshape(dbsz, la, aw), cache_a_v[l].reshape(dbsz, la, aw),
            [bias_s], [pl.BlockSpec(bias_s.shape, lambda i: (0, 0, 0))], a_heads)
        lf3 = lf.reshape(dbsz, dseq, b_heads)
        lf_all = jnp.concatenate([cache_b_lf[l].astype(F32), lf3], axis=1)
        c = _cumsum_lanes(jnp.pad(jnp.swapaxes(lf_all, 1, 2), ((0, 0), (0, 0), (0, lpad))))
        c_col = jnp.swapaxes(c[:, :, past:past + dseq], 1, 2)
        yb = _sample_attention(
            functools.partial(_fox_sample_kernel, nheads=b_heads, head0=a_heads), q3, k3, v3,
            cache_b_k[l].reshape(dbsz, past, bw), cache_b_v[l].reshape(dbsz, past, bw),
            [c_col, c],
            [pl.BlockSpec((1, dseq, b_heads), lambda i: (i, 0, 0)),
             pl.BlockSpec((1, b_heads, c.shape[2]), lambda i: (i, 0, 0))], b_heads)
        k4 = k_f.reshape(dbsz, dseq, a_heads + b_heads, HEAD_DIM)
        v4 = v_f.reshape(dbsz, dseq, a_heads + b_heads, HEAD_DIM)
        outs[2].append(k4[:, :, :a_heads])
        outs[3].append(v4[:, :, :a_heads])
        outs[7].append(k4[:, :, a_heads:])
        outs[8].append(v4[:, :, a_heads:])
        outs[9].append(lf3)
        hs = _merge_ffn(hs, ya.reshape(dbsz * dseq, aw), yb.reshape(dbsz * dseq, bw), gates, wt)

    stacked = [jnp.stack(o) for o in outs]
    return (hp.reshape(bsz, seq, d), hs.reshape(dbsz, dseq, d), *stacked)
```
